```python
import jax, jax.numpy as jnp
from jax import lax
import numpy as np

D_MODEL = 2048
BATCH = 16
SEQ = 2048
DEPTH = 1

PLE_DIM = 256
GLA_HEADS = 4
GLA_DK = D_MODEL // 16
GLA_DV = D_MODEL // 8
GLA_RANK = 16
GLA_TAU = 16.0
GLA_CHUNK = 64
SWA_Q_HEADS = 16
SWA_KV_HEADS = 4
SWA_HEAD_DIM = D_MODEL // 32
WINDOW = 128
D_FF = 4 * D_MODEL
EPS = 1e-6
NEG = -1e30

BRANCH_A_WIDTH = GLA_HEADS * GLA_DV
BRANCH_B_WIDTH = SWA_Q_HEADS * SWA_HEAD_DIM
SPLIT_SIZES = [
    GLA_HEADS * GLA_DK,
    GLA_HEADS * GLA_DK,
    GLA_HEADS * GLA_DV,
    GLA_HEADS * GLA_DV,
    GLA_RANK,
    SWA_Q_HEADS * SWA_HEAD_DIM,
    SWA_KV_HEADS * SWA_HEAD_DIM,
    SWA_KV_HEADS * SWA_HEAD_DIM,
    D_MODEL,
    D_MODEL,
]
IN_WIDTH = sum(SPLIT_SIZES)
SPLITS = [int(s) for s in np.cumsum(SPLIT_SIZES)[:-1]]

kernel_name = "hybrid_gla_swa_sink_alibi_block"


def rmsnorm(x, g):
    xf = x.astype(jnp.float32)
    y = xf * lax.rsqrt(jnp.mean(xf * xf, axis=-1, keepdims=True) + EPS)
    return (y * g.astype(jnp.float32)).astype(x.dtype)


def gla_branch(q, k, v, log_a, r, gn_gain):
    B, S = q.shape[0], q.shape[1]
    nc = S // GLA_CHUNK
    C = GLA_CHUNK

    def to_chunks(t, d):
        return t.astype(jnp.float32).reshape(B, nc, C, GLA_HEADS, d).transpose(1, 0, 3, 2, 4)

    qc = to_chunks(q * (GLA_DK ** -0.5), GLA_DK)
    kc = to_chunks(k, GLA_DK)
    vc = to_chunks(v, GLA_DV)
    gc = to_chunks(log_a, GLA_DK)
    tri = jnp.tril(jnp.ones((C, C), dtype=bool))

    def step(state, inp):
        qi, ki, vi, gi = inp
        b = jnp.cumsum(gi, axis=2)
        o_inter = jnp.einsum('bhcd,bhde->bhce', qi * jnp.exp(b), state)
        diff = b[:, :, :, None, :] - b[:, :, None, :, :]
        decay = jnp.exp(jnp.where(tri[:, :, None], diff, -jnp.inf))
        attn = jnp.einsum('bhid,bhjd,bhijd->bhij', qi, ki, decay)
        o_intra = jnp.einsum('bhij,bhje->bhie', attn, vi)
        b_last = b[:, :, -1:, :]
        new_state = jnp.exp(b_last[:, :, 0, :])[..., None] * state + jnp.einsum(
            'bhcd,bhce->bhde', ki * jnp.exp(b_last - b), vi)
        return new_state, o_inter + o_intra

    state0 = jnp.zeros((B, GLA_HEADS, GLA_DK, GLA_DV), jnp.float32)
    _, o = lax.scan(step, state0, (qc, kc, vc, gc))
    o = o.transpose(1, 0, 3, 2, 4).reshape(B, S, GLA_HEADS, GLA_DV)
    o = o * lax.rsqrt(jnp.mean(o * o, axis=-1, keepdims=True) + EPS) * gn_gain.astype(jnp.float32)
    o = o.reshape(B, S, GLA_HEADS * GLA_DV)
    return (o * jax.nn.silu(r.astype(jnp.float32))).astype(r.dtype)


def swa_branch(q, k, v, sinks):
    B, S = q.shape[0], q.shape[1]
    nb = S // WINDOW
    G = SWA_Q_HEADS // SWA_KV_HEADS
    hd = SWA_HEAD_DIM
    qb = q.reshape(B, nb, WINDOW, SWA_KV_HEADS, G, hd)

    def banded(t):
        tb = t.reshape(B, nb, WINDOW, SWA_KV_HEADS, hd)
        prev = jnp.pad(tb[:, :-1], ((0, 0), (1, 0), (0, 0), (0, 0), (0, 0)))
        return jnp.concatenate([prev, tb], axis=2)

    kb, vb = banded(k), banded(v)
    scores = jnp.einsum('bnikgd,bnjkd->bnkgij', qb, kb).astype(jnp.float32) * (hd ** -0.5)
    qi = jnp.arange(WINDOW)[:, None] + WINDOW
    kj = jnp.arange(2 * WINDOW)[None, :]
    dist = qi - kj
    allowed = (dist >= 0) & (dist < WINDOW)
    blk = jnp.arange(nb)[:, None, None]
    valid = allowed[None] & ((blk > 0) | (kj[None] >= WINDOW))
    slopes = 2.0 ** (-8.0 * jnp.arange(1, SWA_Q_HEADS + 1, dtype=jnp.float32) / SWA_Q_HEADS)
    slopes = slopes.reshape(SWA_KV_HEADS, G)
    bias = -slopes[:, :, None, None] * dist.astype(jnp.float32)[None, None]
    scores = jnp.where(valid[None, :, None, None], scores + bias[None, None], NEG)
    sink = sinks.astype(jnp.float32).reshape(SWA_KV_HEADS, G)[:, :, None, None]
    m = jnp.maximum(jnp.max(scores, axis=-1, keepdims=True), sink)
    e = jnp.exp(scores - m)
    probs = e / (jnp.sum(e, axis=-1, keepdims=True) + jnp.exp(sink - m))
    out = jnp.einsum('bnkgij,bnjkd->bnikgd', probs.astype(v.dtype), vb)
    return out.reshape(B, S, SWA_Q_HEADS * hd)


def setup_inputs(seed: int = 0) -> dict:
    key = jax.random.key(seed)
    ks = jax.random.split(key, 20)
    f32 = jnp.float32

    def nrm(k, shape, scale):
        return jax.random.normal(k, shape, f32) * scale

    def gain(k, shape):
        return 1.0 + 0.05 * jax.random.normal(k, shape, f32)

    return {
        "x": nrm(ks[0], (BATCH, SEQ, D_MODEL), 1.0),
        "p": nrm(ks[1], (DEPTH, BATCH, SEQ, PLE_DIM), 1.0),
        "norm_mix": gain(ks[2], (DEPTH, D_MODEL)),
        "w_in": nrm(ks[3], (DEPTH, D_MODEL, IN_WIDTH), D_MODEL ** -0.5),
        "w_decay": nrm(ks[4], (DEPTH, GLA_RANK, GLA_HEADS * GLA_DK), GLA_RANK ** -0.5),
        "b_decay": nrm(ks[5], (DEPTH, GLA_HEADS * GLA_DK), 0.1),
        "gla_norm": gain(ks[6], (DEPTH, GLA_DV)),
        "attn_sinks": nrm(ks[7], (DEPTH, SWA_Q_HEADS), 0.5),
        "w_branch_a": nrm(ks[8], (DEPTH, BRANCH_A_WIDTH, D_MODEL), BRANCH_A_WIDTH ** -0.5),
        "w_branch_b": nrm(ks[9], (DEPTH, BRANCH_B_WIDTH, D_MODEL), BRANCH_B_WIDTH ** -0.5),
        "w_out": nrm(ks[10], (DEPTH, D_MODEL, D_MODEL), D_MODEL ** -0.5),
        "norm_mlp": gain(ks[11], (DEPTH, D_MODEL)),
        "w_up": nrm(ks[12], (DEPTH, D_MODEL, D_FF), D_MODEL ** -0.5),
        "w_down": nrm(ks[13], (DEPTH, D_FF, D_MODEL), D_FF ** -0.5),
        "norm_ple": gain(ks[14], (DEPTH, D_MODEL)),
        "w_ple_gate": nrm(ks[15], (DEPTH, D_MODEL, D_MODEL), D_MODEL ** -0.5),
        "w_ple_proj": nrm(ks[16], (DEPTH, PLE_DIM, D_MODEL), PLE_DIM ** -0.5),
        "norm_final": gain(ks[17], (D_MODEL,)),
    }


def reference(x, p, norm_mix, w_in, w_decay, b_decay, gla_norm, attn_sinks,
              w_branch_a, w_branch_b, w_out, norm_mlp, w_up, w_down,
              norm_ple, w_ple_gate, w_ple_proj, norm_final):
    B, S = x.shape[0], x.shape[1]
    h = x
    for i in range(DEPTH):
        u = rmsnorm(h, norm_mix[i])
        proj = u @ w_in[i]
        gq, gk, gv, gr, gz, sq, sk, sv, ga, gb = jnp.split(proj, SPLITS, axis=-1)
        log_a = jax.nn.log_sigmoid((gz @ w_decay[i] + b_decay[i]).astype(jnp.float32)) / GLA_TAU
        y_a = gla_branch(
            gq.reshape(B, S, GLA_HEADS, GLA_DK),
            gk.reshape(B, S, GLA_HEADS, GLA_DK),
            gv.reshape(B, S, GLA_HEADS, GLA_DV),
            log_a.reshape(B, S, GLA_HEADS, GLA_DK),
            gr, gla_norm[i])
        y_b = swa_branch(
            sq.reshape(B, S, SWA_Q_HEADS, SWA_HEAD_DIM),
            sk.reshape(B, S, SWA_KV_HEADS, SWA_HEAD_DIM),
            sv.reshape(B, S, SWA_KV_HEADS, SWA_HEAD_DIM),
            attn_sinks[i])
        merged = jax.nn.sigmoid(ga) * (y_a @ w_branch_a[i]) + jax.nn.sigmoid(gb) * (y_b @ w_branch_b[i])
        h = h + merged @ w_out[i]
        hn = rmsnorm(h, norm_mlp[i])
        h = h + jnp.square(jax.nn.relu(hn @ w_up[i])) @ w_down[i]
        hp = rmsnorm(h, norm_ple[i])
        h = h + jax.nn.sigmoid(hp @ w_ple_gate[i]) * (p[i] @ w_ple_proj[i])
    return rmsnorm(h, norm_final)
```

```python
import functools

import numpy as np
import jax
import jax.numpy as jnp
from jax import lax
from jax.experimental import pallas as pl
from jax.experimental.pallas import tpu as pltpu

F32 = jnp.float32
BF16 = jnp.bfloat16

D_MODEL = 2048
PLE_DIM = 256
GLA_HEADS = 4
GLA_DK = 128
GLA_DV = 256
GLA_RANK = 16
GLA_TAU = 16.0
GLA_CHUNK = 64
SWA_Q_HEADS = 16
SWA_KV_HEADS = 4
SWA_HEAD_DIM = 64
WINDOW = 128
D_FF = 4 * D_MODEL
EPS = 1e-6
NEG = -1e30

LANES = 128
VMEM_LIMIT = 56 * 1024 * 1024

COL_GA, COL_GB, COL_GV, COL_GR, COL_SQ, COL_GQ, COL_GK, COL_SK, COL_SV = (
    0, 2048, 4096, 5120, 6144, 7168, 7680, 8192, 8448)
PROJ_W = 8704
_SRC = dict(gq=0, gk=512, gv=1024, gr=2048, gz=3072, sq=3088, sk=4112, sv=4368,
            ga=4624, gb=6672, end=8720)


def _params(*sem):
    return pltpu.CompilerParams(dimension_semantics=sem, vmem_limit_bytes=VMEM_LIMIT)


def _rms_scale(x):
    return lax.rsqrt(jnp.mean(x * x, axis=-1, keepdims=True) + EPS)


INPROJ_TM = 1024
INPROJ_TN = 512
NORM_ROWS = 128


def _inproj_kernel(x_ref, g_ref, w_ref, wz_ref, o_ref, oz_ref, u_ref):
    @pl.when(pl.program_id(1) == 0)
    def _():
        def body(r, c):
            rows = pl.ds(pl.multiple_of(r * NORM_ROWS, NORM_ROWS), NORM_ROWS)
            x = x_ref[rows, :]
            u_ref[rows, :] = (x * _rms_scale(x) * g_ref[...]).astype(BF16)
            return c
        lax.fori_loop(0, x_ref.shape[0] // NORM_ROWS, body, 0)
        oz_ref[...] = jnp.dot(u_ref[...], wz_ref[...], preferred_element_type=F32)

    o_ref[...] = jnp.dot(u_ref[...], w_ref[...],
                         preferred_element_type=F32).astype(BF16)


def _inproj(x2, g, w_main, w_z):
    t = x2.shape[0]
    tm = min(INPROJ_TM, t)
    return pl.pallas_call(
        _inproj_kernel,
        grid=(t // tm, PROJ_W // INPROJ_TN),
        in_specs=[
            pl.BlockSpec((tm, D_MODEL), lambda i, j: (i, 0)),
            pl.BlockSpec((1, D_MODEL), lambda i, j: (0, 0)),
            pl.BlockSpec((D_MODEL, INPROJ_TN), lambda i, j: (0, j)),
            pl.BlockSpec((D_MODEL, LANES), lambda i, j: (0, 0)),
        ],
        out_specs=[
            pl.BlockSpec((tm, INPROJ_TN), lambda i, j: (i, j)),
            pl.BlockSpec((tm, LANES), lambda i, j: (i, 0)),
        ],
        out_shape=[
            jax.ShapeDtypeStruct((t, PROJ_W), BF16),
            jax.ShapeDtypeStruct((t, LANES), F32),
        ],
        scratch_shapes=[pltpu.VMEM((tm, D_MODEL), BF16)],
        compiler_params=_params("parallel", "arbitrary"),
        name="inproj",
    )(x2, g, w_main, w_z)


GLA_LEVELS = tuple(GLA_CHUNK >> (l + 1) for l in range(6))
N_LEV = len(GLA_LEVELS)


def _gla_constants():
    c = GLA_CHUNK
    i = np.arange(c)[:, None]
    t = np.arange(c)[None, :]
    blocks = [(t <= i), (t > i)]
    qmask = []
    pmask = []
    j = np.arange(c)[None, :]
    for h in GLA_LEVELS:
        boundary = (i // (2 * h)) * (2 * h) + h - 1
        is_q = (i % (2 * h)) >= h
        w = np.where(is_q, (t > boundary) & (t <= i), (t > i) & (t <= boundary))
        blocks.append(w)
        qmask.append(np.broadcast_to(is_q, (c, GLA_DK)))
        pmask.append((i // (2 * h) == j // (2 * h)) & ((i % (2 * h)) >= h)
                     & ((j % (2 * h)) < h))
    pmask.append(i == j)
    w_all = np.concatenate(blocks, axis=0).astype(np.float32)
    w_all = np.concatenate([w_all, w_all], axis=1)
    qm = np.stack(qmask).astype(np.float32)
    pm = np.stack(pmask).astype(np.float32)
    return w_all, qm, 1.0 - qm, pm


def _gla_kernel(q_ref, k_ref, v_ref, r_ref, z_ref, wd_ref, bd_ref, gain_ref,
                wall_ref, qm_ref, km_ref, pm_ref, y_ref, la_ref, st_ref):
    s = q_ref.shape[0]
    c = GLA_CHUNK
    z = jnp.dot(z_ref[...].astype(BF16), wd_ref[...],
                preferred_element_type=F32) + bd_ref[...]
    la_ref[...] = (jnp.minimum(z, 0.0) - jnp.log1p(jnp.exp(-jnp.abs(z)))) * (1.0 / GLA_TAU)
    st_ref[...] = jnp.zeros_like(st_ref)
    scale = GLA_DK ** -0.5

    def chunk(ci, carry):
        rows = pl.ds(pl.multiple_of(ci * c, c), c)
        la = la_ref[rows, :]
        hi = la.astype(BF16)
        lo = (la - hi.astype(F32)).astype(BF16)
        e = jnp.exp(jnp.dot(wall_ref[...], jnp.concatenate([hi, lo], axis=0),
                            preferred_element_type=F32))
        e_cum = e[0:c]
        e_rev = e[c:2 * c]
        e_lev = e[2 * c:].reshape(N_LEV, c, GLA_DK)
        qf = q_ref[rows, :].astype(F32) * scale
        kf = k_ref[rows, :].astype(F32)
        v = v_ref[rows, :]
        st = st_ref[...]
        o = lax.dot_general((qf * e_cum).astype(BF16), st.astype(BF16),
                            (((1,), (1,)), ((), ())), preferred_element_type=F32)
        qh = jnp.concatenate([(qf[None] * (e_lev * qm_ref[...])).astype(BF16),
                              qf.astype(BF16)[None]], axis=0)
        kh = jnp.concatenate([(kf[None] * (e_lev * km_ref[...])).astype(BF16),
                              kf.astype(BF16)[None]], axis=0)
        a = lax.dot_general(qh, kh, (((2,), (2,)), ((0,), (0,))),
                            preferred_element_type=F32)
        a = jnp.sum(a * pm_ref[...], axis=0)
        o = o + jnp.dot(a.astype(BF16), v, preferred_element_type=F32)
        o = o * _rms_scale(o) * gain_ref[...]
        r = r_ref[rows, :].astype(F32)
        y_ref[rows, :] = (o * (r * jax.nn.sigmoid(r))).astype(y_ref.dtype)
        upd = lax.dot_general(v, (kf * e_rev).astype(BF16),
                              (((0,), (0,)), ((), ())), preferred_element_type=F32)
        st_ref[...] = st * e_cum[c - 1:c, :] + upd
        return carry

    lax.fori_loop(0, s // c, chunk, 0)


def _gla(proj3, gz3, wd, bd, gain):
    b, s, _ = proj3.shape
    w_all, qm, km, pm = _gla_constants()
    const = lambda shape: pl.BlockSpec(shape, lambda bi, hi: (0,) * len(shape))
    return pl.pallas_call(
        _gla_kernel,
        grid=(b, GLA_HEADS),
        in_specs=[
            pl.BlockSpec((None, s, GLA_DK), lambda bi, hi: (bi, 0, COL_GQ // GLA_DK + hi)),
            pl.BlockSpec((None, s, GLA_DK), lambda bi, hi: (bi, 0, COL_GK // GLA_DK + hi)),
            pl.BlockSpec((None, s, GLA_DV), lambda bi, hi: (bi, 0, COL_GV // GLA_DV + hi)),
            pl.BlockSpec((None, s, GLA_DV), lambda bi, hi: (bi, 0, COL_GR // GLA_DV + hi)),
            pl.BlockSpec((None, s, LANES), lambda bi, hi: (bi, 0, 0)),
            pl.BlockSpec((None, LANES, GLA_DK), lambda bi, hi: (hi, 0, 0)),
            pl.BlockSpec((None, 1, GLA_DK), lambda bi, hi: (hi, 0, 0)),
            const((1, GLA_DV)),
            const(w_all.shape), const(qm.shape), const(km.shape), const(pm.shape),
        ],
        out_specs=pl.BlockSpec((None, s, GLA_DV), lambda bi, hi: (bi, 0, hi)),
        out_shape=jax.ShapeDtypeStruct((b, s, GLA_HEADS * GLA_DV), BF16),
        scratch_shapes=[pltpu.VMEM((s, GLA_DK), F32),
                        pltpu.VMEM((GLA_DV, GLA_DK), F32)],
        compiler_params=_params("parallel", "parallel"),
        name="gla",
    )(proj3, proj3, proj3, proj3, gz3, wd, bd, gain,
      jnp.asarray(w_all, BF16), jnp.asarray(qm), jnp.asarray(km), jnp.asarray(pm))


HALF = LANES // 2
GROUP = SWA_Q_HEADS // SWA_KV_HEADS


def _swa_kernel(sink_ref, q_ref, kp_ref, kc_ref, vp_ref, vc_ref, o_ref):
    n = pl.program_id(1)
    w = WINDOW
    qi = lax.broadcasted_iota(jnp.int32, (w, w), 0)
    kj = lax.broadcasted_iota(jnp.int32, (w, w), 1)
    valid_c = kj <= qi
    valid_p = jnp.logical_and(kj > qi, n > 0)
    dist_c = (qi - kj).astype(F32)
    dist_p = (qi + w - kj).astype(F32)
    lane = lax.broadcasted_iota(jnp.int32, (w, LANES), 1)
    lo_lanes = lane < HALF

    def halves(ref, t):
        x = ref[:, t * LANES:(t + 1) * LANES].astype(F32)
        sw = pltpu.roll(x, HALF, axis=1)
        zero = jnp.zeros_like(x)
        even = (jnp.where(lo_lanes, x, zero).astype(BF16),
                jnp.where(lo_lanes, zero, sw).astype(BF16))
        odd = (jnp.where(lo_lanes, sw, zero).astype(BF16),
               jnp.where(lo_lanes, zero, x).astype(BF16))
        return even, odd

    def per_kv(ref):
        out = []
        for t in range(SWA_KV_HEADS // 2):
            out.extend(halves(ref, t))
        return out

    kp, kc, vp, vc = per_kv(kp_ref), per_kv(kc_ref), per_kv(vp_ref), per_kv(vc_ref)
    nt = (((1,), (1,)), ((), ()))
    for pair in range(SWA_Q_HEADS // 2):
        qp = q_ref[:, pair * LANES:(pair + 1) * LANES]
        acc = None
        for half in range(2):
            h = 2 * pair + half
            g = h // GROUP
            slope = 2.0 ** (-8.0 * (h + 1) / SWA_Q_HEADS)
            sink = sink_ref[h]
            sp = lax.dot_general(qp, kp[g][half], nt, preferred_element_type=F32)
            sc = lax.dot_general(qp, kc[g][half], nt, preferred_element_type=F32)
            sp = jnp.where(valid_p, sp * (SWA_HEAD_DIM ** -0.5) - slope * dist_p, NEG)
            sc = jnp.where(valid_c, sc * (SWA_HEAD_DIM ** -0.5) - slope * dist_c, NEG)
            m = jnp.maximum(jnp.maximum(jnp.max(sp, axis=-1, keepdims=True),
                                        jnp.max(sc, axis=-1, keepdims=True)), sink)
            ep = jnp.exp(sp - m)
            ec = jnp.exp(sc - m)
            den = (jnp.sum(ep, axis=-1, keepdims=True) + jnp.sum(ec, axis=-1, keepdims=True)
                   + jnp.exp(sink - m))
            inv = 1.0 / den
            part = (jnp.dot((ep * inv).astype(BF16), vp[g][half], preferred_element_type=F32)
                    + jnp.dot((ec * inv).astype(BF16), vc[g][half], preferred_element_type=F32))
            acc = part if acc is None else acc + part
        o_ref[:, pair * LANES:(pair + 1) * LANES] = acc.astype(o_ref.dtype)


def _swa(proj3, sinks):
    b, s, _ = proj3.shape
    w = WINDOW
    qw = SWA_Q_HEADS * SWA_HEAD_DIM
    kvw = SWA_KV_HEADS * SWA_HEAD_DIM
    prev = lambda n: jnp.maximum(n - 1, 0)
    return pl.pallas_call(
        _swa_kernel,
        grid=(b, s // w),
        in_specs=[
            pl.BlockSpec(memory_space=pltpu.SMEM),
            pl.BlockSpec((None, w, qw), lambda bi, n: (bi, n, COL_SQ // qw)),
            pl.BlockSpec((None, w, kvw), lambda bi, n: (bi, prev(n), COL_SK // kvw)),
            pl.BlockSpec((None, w, kvw), lambda bi, n: (bi, n, COL_SK // kvw)),
            pl.BlockSpec((None, w, kvw), lambda bi, n: (bi, prev(n), COL_SV // kvw)),
            pl.BlockSpec((None, w, kvw), lambda bi, n: (bi, n, COL_SV // kvw)),
        ],
        out_specs=pl.BlockSpec((None, w, qw), lambda bi, n: (bi, n, 0)),
        out_shape=jax.ShapeDtypeStruct((b, s, qw), BF16),
        compiler_params=_params("parallel", "parallel"),
        name="swa",
    )(sinks, proj3, proj3, proj3, proj3, proj3)


MERGE_TM = 256


def _merge_kernel(ya_ref, yb_ref, ga_ref, gb_ref, x_ref, wa_ref, wb_ref, wo_ref, o_ref):
    a = jnp.dot(ya_ref[...], wa_ref[...], preferred_element_type=F32)
    bb = jnp.dot(yb_ref[...], wb_ref[...], preferred_element_type=F32)
    merged = (jax.nn.sigmoid(ga_ref[...].astype(F32)) * a
              + jax.nn.sigmoid(gb_ref[...].astype(F32)) * bb)
    o_ref[...] = x_ref[...] + jnp.dot(merged.astype(BF16), wo_ref[...],
                                      preferred_element_type=F32)


def _resident(shape):
    return pl.BlockSpec(shape, lambda i: (0,) * len(shape), pipeline_mode=pl.Buffered(1))


def _merge(ya, yb, proj, x2, wa, wb, wo):
    t = x2.shape[0]
    tm = MERGE_TM
    wa_w = GLA_HEADS * GLA_DV
    wb_w = SWA_Q_HEADS * SWA_HEAD_DIM
    return pl.pallas_call(
        _merge_kernel,
        grid=(t // tm,),
        in_specs=[
            pl.BlockSpec((tm, wa_w), lambda i: (i, 0)),
            pl.BlockSpec((tm, wb_w), lambda i: (i, 0)),
            pl.BlockSpec((tm, D_MODEL), lambda i: (i, COL_GA // D_MODEL)),
            pl.BlockSpec((tm, D_MODEL), lambda i: (i, COL_GB // D_MODEL)),
            pl.BlockSpec((tm, D_MODEL), lambda i: (i, 0)),
            _resident((wa_w, D_MODEL)),
            _resident((wb_w, D_MODEL)),
            _resident((D_MODEL, D_MODEL)),
        ],
        out_specs=pl.BlockSpec((tm, D_MODEL), lambda i: (i, 0)),
        out_shape=jax.ShapeDtypeStruct((t, D_MODEL), F32),
        compiler_params=_params("parallel"),
        name="merge",
    )(ya, yb, proj, proj, x2, wa, wb, wo)


MLP_TM = 1024
MLP_TF = 512


def _mlp_kernel(h_ref, g_ref, wu_ref, wd_ref, o_ref, hn_ref):
    @pl.when(pl.program_id(1) == 0)
    def _():
        def body(r, c):
            rows = pl.ds(pl.multiple_of(r * NORM_ROWS, NORM_ROWS), NORM_ROWS)
            x = h_ref[rows, :]
            hn_ref[rows, :] = (x * _rms_scale(x) * g_ref[...]).astype(BF16)
            o_ref[rows, :] = x
            return c
        lax.fori_loop(0, h_ref.shape[0] // NORM_ROWS, body, 0)

    up = jnp.dot(hn_ref[...], wu_ref[...], preferred_element_type=F32)
    act = jnp.square(jnp.maximum(up, 0.0)).astype(BF16)
    o_ref[...] += jnp.dot(act, wd_ref[...], preferred_element_type=F32)


def _mlp(h1, g, wu, wd):
    t = h1.shape[0]
    tm = min(MLP_TM, t)
    return pl.pallas_call(
        _mlp_kernel,
        grid=(t // tm, D_FF // MLP_TF),
        in_specs=[
            pl.BlockSpec((tm, D_MODEL), lambda i, j: (i, 0)),
            pl.BlockSpec((1, D_MODEL), lambda i, j: (0, 0)),
            pl.BlockSpec((D_MODEL, MLP_TF), lambda i, j: (0, j)),
            pl.BlockSpec((MLP_TF, D_MODEL), lambda i, j: (j, 0)),
        ],
        out_specs=pl.BlockSpec((tm, D_MODEL), lambda i, j: (i, 0)),
        out_shape=jax.ShapeDtypeStruct((t, D_MODEL), F32),
        scratch_shapes=[pltpu.VMEM((tm, D_MODEL), BF16)],
        compiler_params=_params("parallel", "arbitrary"),
        name="mlp",
    )(h1, g, wu, wd)


PLE_TM = 256


def _ple_kernel(h_ref, p_ref, g_ref, gf_ref, wg_ref, wp_ref, o_ref):
    h = h_ref[...]
    hp = (h * _rms_scale(h) * g_ref[...]).astype(BF16)
    gate = jax.nn.sigmoid(jnp.dot(hp, wg_ref[...], preferred_element_type=F32))
    emb = jnp.dot(p_ref[...].astype(BF16), wp_ref[...], preferred_element_type=F32)
    h = h + gate * emb
    o_ref[...] = h * _rms_scale(h) * gf_ref[...]


def _ple(h2, p2, g, gf, wg, wp):
    t = h2.shape[0]
    tm = PLE_TM
    return pl.pallas_call(
        _ple_kernel,
        grid=(t // tm,),
        in_specs=[
            pl.BlockSpec((tm, D_MODEL), lambda i: (i, 0)),
            pl.BlockSpec((tm, PLE_DIM), lambda i: (i, 0)),
            _resident((1, D_MODEL)),
            _resident((1, D_MODEL)),
            _resident((D_MODEL, D_MODEL)),
            _resident((PLE_DIM, D_MODEL)),
        ],
        out_specs=pl.BlockSpec((tm, D_MODEL), lambda i: (i, 0)),
        out_shape=jax.ShapeDtypeStruct((t, D_MODEL), F32),
        compiler_params=_params("parallel"),
        name="ple",
    )(h2, p2, g, gf, wg, wp)


def _layer(h2d, p2d, b, s, norm_mix, w_in, w_decay, b_decay, gla_norm, attn_sinks,
           w_branch_a, w_branch_b, w_out, norm_mlp, w_up, w_down, norm_ple,
           w_ple_gate, w_ple_proj, final_gain):
    src = _SRC
    seg = lambda a, z: w_in[:, src[a]:src[z]]
    w_main = jnp.concatenate(
        [seg("ga", "gb"), seg("gb", "end"), seg("gv", "gr"), seg("gr", "gz"),
         seg("sq", "sk"), seg("gq", "gk"), seg("gk", "gv"), seg("sk", "sv"),
         seg("sv", "ga")], axis=1).astype(BF16)
    w_z = jnp.pad(seg("gz", "sq"), ((0, 0), (0, LANES - GLA_RANK))).astype(BF16)
    proj, gz = _inproj(h2d, norm_mix.reshape(1, D_MODEL), w_main, w_z)
    proj3 = proj.reshape(b, s, PROJ_W)
    gz3 = gz.reshape(b, s, LANES)

    wd = jnp.pad(w_decay.reshape(GLA_RANK, GLA_HEADS, GLA_DK).transpose(1, 0, 2),
                 ((0, 0), (0, LANES - GLA_RANK), (0, 0))).astype(BF16)
    bd = b_decay.reshape(GLA_HEADS, 1, GLA_DK)
    ya = _gla(proj3, gz3, wd, bd, gla_norm.reshape(1, GLA_DV))
    yb = _swa(proj3, attn_sinks)

    t = b * s
    h1 = _merge(ya.reshape(t, -1), yb.reshape(t, -1), proj, h2d,
                w_branch_a.astype(BF16), w_branch_b.astype(BF16), w_out.astype(BF16))
    h2 = _mlp(h1, norm_mlp.reshape(1, D_MODEL), w_up.astype(BF16), w_down.astype(BF16))
    return _ple(h2, p2d, norm_ple.reshape(1, D_MODEL), final_gain,
                w_ple_gate.astype(BF16), w_ple_proj.astype(BF16))


def kernel(x, p, norm_mix, w_in, w_decay, b_decay, gla_norm, attn_sinks, w_branch_a,
           w_branch_b, w_out, norm_mlp, w_up, w_down, norm_ple, w_ple_gate, w_ple_proj,
           norm_final):
    b, s, d = x.shape
    depth = w_in.shape[0]
    assert depth == 1 and d == D_MODEL
    h = _layer(x.reshape(b * s, d), p[0].reshape(b * s, PLE_DIM), b, s,
               norm_mix[0], w_in[0], w_decay[0], b_decay[0], gla_norm[0], attn_sinks[0],
               w_branch_a[0], w_branch_b[0], w_out[0], norm_mlp[0], w_up[0], w_down[0],
               norm_ple[0], w_ple_gate[0], w_ple_proj[0],
               norm_final.reshape(1, D_MODEL))
    return h.reshape(b, s, d)
```

```python
import numpy as np
import jax
import jax.numpy as jnp
from jax import lax
from jax.experimental import pallas as pl
from jax.experimental.pallas import tpu as pltpu

F32 = jnp.float32
BF16 = jnp.bfloat16

D_MODEL = 2048
PLE_DIM = 256
GLA_HEADS = 4
GLA_DK = 128
GLA_DV = 256
GLA_RANK = 16
GLA_TAU = 16.0
GLA_CHUNK = 64
SWA_Q_HEADS = 16
SWA_KV_HEADS = 4
SWA_HEAD_DIM = 64
WINDOW = 128
D_FF = 4 * D_MODEL
EPS = 1e-6
NEG = -1e30

LANES = 128
VMEM_LIMIT = 56 * 1024 * 1024

COL_GA, COL_GB, COL_GV, COL_GR, COL_SQ, COL_GQ, COL_GK, COL_SK, COL_SV = (
    0, 2048, 4096, 5120, 6144, 7168, 7680, 8192, 8448)
PROJ_W = 8704
_SRC = dict(gq=0, gk=512, gv=1024, gr=2048, gz=3072, sq=3088, sk=4112, sv=4368,
            ga=4624, gb=6672, end=8720)


def _params(*sem):
    return pltpu.CompilerParams(dimension_semantics=sem, vmem_limit_bytes=VMEM_LIMIT)


def _rms_scale(x):
    return lax.rsqrt(jnp.mean(x * x, axis=-1, keepdims=True) + EPS)


INPROJ_TM = 1024
INPROJ_TN = 512
NORM_ROWS = 128


def _inproj_kernel(x_ref, g_ref, w_ref, wz_ref, o_ref, oz_ref, u_ref):
    @pl.when(pl.program_id(1) == 0)
    def _():
        def body(r, c):
            rows = pl.ds(pl.multiple_of(r * NORM_ROWS, NORM_ROWS), NORM_ROWS)
            x = x_ref[rows, :]
            u_ref[rows, :] = (x * _rms_scale(x) * g_ref[...]).astype(BF16)
            return c
        lax.fori_loop(0, x_ref.shape[0] // NORM_ROWS, body, 0)
        oz_ref[...] = jnp.dot(u_ref[...], wz_ref[...], preferred_element_type=F32)

    o_ref[...] = jnp.dot(u_ref[...], w_ref[...],
                         preferred_element_type=F32).astype(BF16)


def _inproj(x2, g, w_main, w_z):
    t = x2.shape[0]
    tm = min(INPROJ_TM, t)
    return pl.pallas_call(
        _inproj_kernel,
        grid=(t // tm, PROJ_W // INPROJ_TN),
        in_specs=[
            pl.BlockSpec((tm, D_MODEL), lambda i, j: (i, 0)),
            pl.BlockSpec((1, D_MODEL), lambda i, j: (0, 0)),
            pl.BlockSpec((D_MODEL, INPROJ_TN), lambda i, j: (0, j)),
            pl.BlockSpec((D_MODEL, LANES), lambda i, j: (0, 0)),
        ],
        out_specs=[
            pl.BlockSpec((tm, INPROJ_TN), lambda i, j: (i, j)),
            pl.BlockSpec((tm, LANES), lambda i, j: (i, 0)),
        ],
        out_shape=[
            jax.ShapeDtypeStruct((t, PROJ_W), BF16),
            jax.ShapeDtypeStruct((t, LANES), F32),
        ],
        scratch_shapes=[pltpu.VMEM((tm, D_MODEL), BF16)],
        compiler_params=_params("parallel", "arbitrary"),
        name="inproj",
    )(x2, g, w_main, w_z)


GLA_LEVELS = tuple(GLA_CHUNK >> (l + 1) for l in range(6))
N_LEV = len(GLA_LEVELS)
GLA_GROUP = 8
LOG2E = float(np.log2(np.e))


def _gla_constants():
    c = GLA_CHUNK
    i = np.arange(c)[:, None]
    t = np.arange(c)[None, :]
    j = t
    blocks = [(t <= i), (t > i)]
    pmask = []
    for h in GLA_LEVELS:
        boundary = (i // (2 * h)) * (2 * h) + h - 1
        is_q = (i % (2 * h)) >= h
        blocks.append(np.where(is_q, (t > boundary) & (t <= i), (t > i) & (t <= boundary)))
        pmask.append((i // (2 * h) == j // (2 * h)) & is_q & ((j % (2 * h)) < h))
    pmask.append(i == j)
    w_all = np.concatenate(blocks, axis=0).astype(np.float32)
    w_all = np.concatenate([w_all, w_all], axis=1)
    return w_all, np.stack(pmask).astype(np.float32)


def _gla_kernel(q_ref, k_ref, v_ref, r_ref, z_ref, wd_ref, bd_ref, gain_ref,
                wall_ref, pm_ref, y_ref, la_ref, st_ref):
    s = q_ref.shape[0]
    c = GLA_CHUNK
    g_n = min(GLA_GROUP, s // c)
    rows_n = g_n * c
    z = jnp.dot(z_ref[...].astype(BF16), wd_ref[...],
                preferred_element_type=F32) + bd_ref[...]
    la_ref[...] = ((jnp.minimum(z, 0.0) - jnp.log1p(jnp.exp(-jnp.abs(z))))
                   * (LOG2E / GLA_TAU))
    st_ref[...] = jnp.zeros_like(st_ref)
    scale = GLA_DK ** -0.5
    nt_b = (((2,), (2,)), ((0,), (0,)))

    def group(gi, carry):
        rows = pl.ds(pl.multiple_of(gi * rows_n, rows_n), rows_n)
        la = la_ref[rows, :]
        hi = la.astype(BF16)
        lo = (la - hi.astype(F32)).astype(BF16)
        e = jnp.stack([
            jnp.exp2(jnp.dot(wall_ref[...],
                             jnp.concatenate([hi[g * c:(g + 1) * c], lo[g * c:(g + 1) * c]],
                                             axis=0),
                             preferred_element_type=F32))
            for g in range(g_n)])
        e_cum = e[:, 0:c]
        e_rev = e[:, c:2 * c]
        e_lev = e[:, 2 * c:].reshape(g_n, N_LEV, c, GLA_DK)
        q3 = (q_ref[rows, :].astype(F32) * scale).reshape(g_n, c, GLA_DK)
        k3 = k_ref[rows, :].astype(F32).reshape(g_n, c, GLA_DK)
        v3 = v_ref[rows, :].reshape(g_n, c, GLA_DV)
        qh = jnp.concatenate([(q3[:, None] * e_lev).astype(BF16),
                              q3.astype(BF16)[:, None]], axis=1)
        kh = jnp.concatenate([(k3[:, None] * e_lev).astype(BF16),
                              k3.astype(BF16)[:, None]], axis=1)
        a = lax.dot_general(qh.reshape(g_n * (N_LEV + 1), c, GLA_DK),
                            kh.reshape(g_n * (N_LEV + 1), c, GLA_DK), nt_b,
                            preferred_element_type=F32)
        a = jnp.sum(a.reshape(g_n, N_LEV + 1, c, c) * pm_ref[...][None], axis=1)
        kd = (k3 * e_rev).astype(BF16)
        st = st_ref[...]
        states = []
        for g in range(g_n):
            states.append(st.astype(BF16))
            upd = lax.dot_general(v3[g], kd[g], (((0,), (0,)), ((), ())),
                                  preferred_element_type=F32)
            st = st * e_cum[g, c - 1:c, :] + upd
        st_ref[...] = st
        o = (lax.dot_general((q3 * e_cum).astype(BF16), jnp.stack(states), nt_b,
                             preferred_element_type=F32)
             + lax.dot_general(a.astype(BF16), v3, (((2,), (1,)), ((0,), (0,))),
                               preferred_element_type=F32)).reshape(rows_n, GLA_DV)
        o = o * _rms_scale(o) * gain_ref[...]
        r = r_ref[rows, :].astype(F32)
        y_ref[rows, :] = (o * (r * jax.nn.sigmoid(r))).astype(y_ref.dtype)
        return carry

    lax.fori_loop(0, s // rows_n, group, 0)


def _gla(proj3, gz3, wd, bd, gain):
    b, s, _ = proj3.shape
    w_all, pm = _gla_constants()
    const = lambda shape: pl.BlockSpec(shape, lambda bi, hi: (0,) * len(shape))
    return pl.pallas_call(
        _gla_kernel,
        grid=(b, GLA_HEADS),
        in_specs=[
            pl.BlockSpec((None, s, GLA_DK), lambda bi, hi: (bi, 0, COL_GQ // GLA_DK + hi)),
            pl.BlockSpec((None, s, GLA_DK), lambda bi, hi: (bi, 0, COL_GK // GLA_DK + hi)),
            pl.BlockSpec((None, s, GLA_DV), lambda bi, hi: (bi, 0, COL_GV // GLA_DV + hi)),
            pl.BlockSpec((None, s, GLA_DV), lambda bi, hi: (bi, 0, COL_GR // GLA_DV + hi)),
            pl.BlockSpec((None, s, LANES), lambda bi, hi: (bi, 0, 0)),
            pl.BlockSpec((None, LANES, GLA_DK), lambda bi, hi: (hi, 0, 0)),
            pl.BlockSpec((None, 1, GLA_DK), lambda bi, hi: (hi, 0, 0)),
            const((1, GLA_DV)),
            const(w_all.shape), const(pm.shape),
        ],
        out_specs=pl.BlockSpec((None, s, GLA_DV), lambda bi, hi: (bi, 0, hi)),
        out_shape=jax.ShapeDtypeStruct((b, s, GLA_HEADS * GLA_DV), BF16),
        scratch_shapes=[pltpu.VMEM((s, GLA_DK), F32),
                        pltpu.VMEM((GLA_DV, GLA_DK), F32)],
        compiler_params=_params("parallel", "parallel"),
        name="gla",
    )(proj3, proj3, proj3, proj3, gz3, wd, bd, gain,
      jnp.asarray(w_all, BF16), jnp.asarray(pm))


GROUP = SWA_Q_HEADS // SWA_KV_HEADS
SWA_KEYS = 2 * WINDOW


def _swa_bias():
    qi = np.arange(WINDOW)[None, :] + WINDOW
    kj = np.arange(SWA_KEYS)[:, None]
    dist = (qi - kj).astype(np.float64)
    allowed = (dist >= 0) & (dist < WINDOW)
    slopes = 2.0 ** (-8.0 * np.arange(1, SWA_Q_HEADS + 1) / SWA_Q_HEADS)
    bias = -slopes[:, None, None] * dist[None]
    later = np.where(allowed[None], bias, NEG)
    first = np.where((allowed & (kj >= WINDOW))[None], bias, NEG)
    return np.stack([first, later]).astype(np.float32)


def _swa_kernel(sink_ref, q_ref, kp_ref, kc_ref, vp_ref, vc_ref, bias_ref, o_ref):
    hd = SWA_HEAD_DIM
    q_t = (q_ref[...] * (hd ** -0.5)).T
    k2 = jnp.concatenate([kp_ref[...], kc_ref[...]], axis=0)
    v_t = jnp.concatenate([vp_ref[...], vc_ref[...]], axis=0).T
    zeros = jnp.zeros((hd, WINDOW), BF16)
    scores = []
    for h in range(SWA_Q_HEADS):
        g = h // GROUP
        q_h = q_t[h * hd:(h + 1) * hd]
        rhs = jnp.concatenate([q_h, zeros] if g % 2 == 0 else [zeros, q_h], axis=0)
        scores.append(jnp.dot(k2[:, (g // 2) * LANES:(g // 2 + 1) * LANES], rhs,
                              preferred_element_type=F32) + bias_ref[h])
    probs, inv_dens = [], []
    for h in range(SWA_Q_HEADS):
        sink = sink_ref[h]
        m = jnp.maximum(jnp.max(scores[h], axis=0, keepdims=True), sink)
        e = jnp.exp(scores[h] - m)
        inv_dens.append(1.0 / (jnp.sum(e, axis=0, keepdims=True) + jnp.exp(sink - m)))
        probs.append(e.astype(BF16))
    outs = []
    for h in range(SWA_Q_HEADS):
        g = h // GROUP
        outs.append(jnp.dot(v_t[g * hd:(g + 1) * hd], probs[h],
                            preferred_element_type=F32) * inv_dens[h])
    for pair in range(SWA_Q_HEADS // 2):
        tile = jnp.concatenate(outs[2 * pair:2 * pair + 2], axis=0)
        o_ref[:, pair * LANES:(pair + 1) * LANES] = tile.T.astype(o_ref.dtype)


def _swa(proj3, sinks):
    b, s, _ = proj3.shape
    w = WINDOW
    qw = SWA_Q_HEADS * SWA_HEAD_DIM
    kvw = SWA_KV_HEADS * SWA_HEAD_DIM
    prev = lambda n: jnp.maximum(n - 1, 0)
    return pl.pallas_call(
        _swa_kernel,
        grid=(b, s // w),
        in_specs=[
            pl.BlockSpec(memory_space=pltpu.SMEM),
            pl.BlockSpec((None, w, qw), lambda bi, n: (bi, n, COL_SQ // qw)),
            pl.BlockSpec((None, w, kvw), lambda bi, n: (bi, prev(n), COL_SK // kvw)),
            pl.BlockSpec((None, w, kvw), lambda bi, n: (bi, n, COL_SK // kvw)),
            pl.BlockSpec((None, w, kvw), lambda bi, n: (bi, prev(n), COL_SV // kvw)),
            pl.BlockSpec((None, w, kvw), lambda bi, n: (bi, n, COL_SV // kvw)),
            pl.BlockSpec((None, SWA_Q_HEADS, SWA_KEYS, w),
                         lambda bi, n: (jnp.minimum(n, 1), 0, 0, 0)),
        ],
        out_specs=pl.BlockSpec((None, w, qw), lambda bi, n: (bi, n, 0)),
        out_shape=jax.ShapeDtypeStruct((b, s, qw), BF16),
        compiler_params=_params("parallel", "parallel"),
        name="swa",
    )(sinks, proj3, proj3, proj3, proj3, proj3, jnp.asarray(_swa_bias()))


MERGE_TM = 256


def _merge_kernel(ya_ref, yb_ref, ga_ref, gb_ref, x_ref, wa_ref, wb_ref, wo_ref, o_ref):
    a = jnp.dot(ya_ref[...], wa_ref[...], preferred_element_type=F32)
    bb = jnp.dot(yb_ref[...], wb_ref[...], preferred_element_type=F32)
    merged = (jax.nn.sigmoid(ga_ref[...].astype(F32)) * a
              + jax.nn.sigmoid(gb_ref[...].astype(F32)) * bb)
    o_ref[...] = x_ref[...] + jnp.dot(merged.astype(BF16), wo_ref[...],
                                      preferred_element_type=F32)


def _resident(shape):
    return pl.BlockSpec(shape, lambda i: (0,) * len(shape), pipeline_mode=pl.Buffered(1))


def _merge(ya, yb, proj, x2, wa, wb, wo):
    t = x2.shape[0]
    tm = MERGE_TM
    wa_w = GLA_HEADS * GLA_DV
    wb_w = SWA_Q_HEADS * SWA_HEAD_DIM
    return pl.pallas_call(
        _merge_kernel,
        grid=(t // tm,),
        in_specs=[
            pl.BlockSpec((tm, wa_w), lambda i: (i, 0)),
            pl.BlockSpec((tm, wb_w), lambda i: (i, 0)),
            pl.BlockSpec((tm, D_MODEL), lambda i: (i, COL_GA // D_MODEL)),
            pl.BlockSpec((tm, D_MODEL), lambda i: (i, COL_GB // D_MODEL)),
            pl.BlockSpec((tm, D_MODEL), lambda i: (i, 0)),
            _resident((wa_w, D_MODEL)),
            _resident((wb_w, D_MODEL)),
            _resident((D_MODEL, D_MODEL)),
        ],
        out_specs=pl.BlockSpec((tm, D_MODEL), lambda i: (i, 0)),
        out_shape=jax.ShapeDtypeStruct((t, D_MODEL), F32),
        compiler_params=_params("parallel"),
        name="merge",
    )(ya, yb, proj, proj, x2, wa, wb, wo)


MLP_TM = 1024
MLP_TF = 512


def _mlp_kernel(h_ref, g_ref, wu_ref, wd_ref, o_ref, hn_ref):
    @pl.when(pl.program_id(1) == 0)
    def _():
        def body(r, c):
            rows = pl.ds(pl.multiple_of(r * NORM_ROWS, NORM_ROWS), NORM_ROWS)
            x = h_ref[rows, :]
            hn_ref[rows, :] = (x * _rms_scale(x) * g_ref[...]).astype(BF16)
            o_ref[rows, :] = x
            return c
        lax.fori_loop(0, h_ref.shape[0] // NORM_ROWS, body, 0)

    up = jnp.dot(hn_ref[...], wu_ref[...], preferred_element_type=F32)
    act = jnp.square(jnp.maximum(up, 0.0)).astype(BF16)
    o_ref[...] += jnp.dot(act, wd_ref[...], preferred_element_type=F32)


def _mlp(h1, g, wu, wd):
    t = h1.shape[0]
    tm = min(MLP_TM, t)
    return pl.pallas_call(
        _mlp_kernel,
        grid=(t // tm, D_FF // MLP_TF),
        in_specs=[
            pl.BlockSpec((tm, D_MODEL), lambda i, j: (i, 0)),
            pl.BlockSpec((1, D_MODEL), lambda i, j: (0, 0)),
            pl.BlockSpec((D_MODEL, MLP_TF), lambda i, j: (0, j)),
            pl.BlockSpec((MLP_TF, D_MODEL), lambda i, j: (j, 0)),
        ],
        out_specs=pl.BlockSpec((tm, D_MODEL), lambda i, j: (i, 0)),
        out_shape=jax.ShapeDtypeStruct((t, D_MODEL), F32),
        scratch_shapes=[pltpu.VMEM((tm, D_MODEL), BF16)],
        compiler_params=_params("parallel", "arbitrary"),
        name="mlp",
    )(h1, g, wu, wd)


PLE_TM = 256


def _ple_kernel(h_ref, p_ref, g_ref, gf_ref, wg_ref, wp_ref, o_ref):
    h = h_ref[...]
    hp = (h * _rms_scale(h) * g_ref[...]).astype(BF16)
    gate = jax.nn.sigmoid(jnp.dot(hp, wg_ref[...], preferred_element_type=F32))
    emb = jnp.dot(p_ref[...].astype(BF16), wp_ref[...], preferred_element_type=F32)
    h = h + gate * emb
    o_ref[...] = h * _rms_scale(h) * gf_ref[...]


def _ple(h2, p2, g, gf, wg, wp):
    t = h2.shape[0]
    tm = PLE_TM
    return pl.pallas_call(
        _ple_kernel,
        grid=(t // tm,),
        in_specs=[
            pl.BlockSpec((tm, D_MODEL), lambda i: (i, 0)),
            pl.BlockSpec((tm, PLE_DIM), lambda i: (i, 0)),
            _resident((1, D_MODEL)),
            _resident((1, D_MODEL)),
            _resident((D_MODEL, D_MODEL)),
            _resident((PLE_DIM, D_MODEL)),
        ],
        out_specs=pl.BlockSpec((tm, D_MODEL), lambda i: (i, 0)),
        out_shape=jax.ShapeDtypeStruct((t, D_MODEL), F32),
        compiler_params=_params("parallel"),
        name="ple",
    )(h2, p2, g, gf, wg, wp)


def _layer(h2d, p2d, b, s, norm_mix, w_in, w_decay, b_decay, gla_norm, attn_sinks,
           w_branch_a, w_branch_b, w_out, norm_mlp, w_up, w_down, norm_ple,
           w_ple_gate, w_ple_proj, final_gain):
    src = _SRC
    seg = lambda a, z: w_in[:, src[a]:src[z]]
    w_main = jnp.concatenate(
        [seg("ga", "gb"), seg("gb", "end"), seg("gv", "gr"), seg("gr", "gz"),
         seg("sq", "sk"), seg("gq", "gk"), seg("gk", "gv"), seg("sk", "sv"),
         seg("sv", "ga")], axis=1).astype(BF16)
    w_z = jnp.pad(seg("gz", "sq"), ((0, 0), (0, LANES - GLA_RANK))).astype(BF16)
    proj, gz = _inproj(h2d, norm_mix.reshape(1, D_MODEL), w_main, w_z)
    proj3 = proj.reshape(b, s, PROJ_W)
    gz3 = gz.reshape(b, s, LANES)

    wd = jnp.pad(w_decay.reshape(GLA_RANK, GLA_HEADS, GLA_DK).transpose(1, 0, 2),
                 ((0, 0), (0, LANES - GLA_RANK), (0, 0))).astype(BF16)
    bd = b_decay.reshape(GLA_HEADS, 1, GLA_DK)
    ya = _gla(proj3, gz3, wd, bd, gla_norm.reshape(1, GLA_DV))
    yb = _swa(proj3, attn_sinks)

    t = b * s
    h1 = _merge(ya.reshape(t, -1), yb.reshape(t, -1), proj, h2d,
                w_branch_a.astype(BF16), w_branch_b.astype(BF16), w_out.astype(BF16))
    h2 = _mlp(h1, norm_mlp.reshape(1, D_MODEL), w_up.astype(BF16), w_down.astype(BF16))
    return _ple(h2, p2d, norm_ple.reshape(1, D_MODEL), final_gain,
                w_ple_gate.astype(BF16), w_ple_proj.astype(BF16))


def kernel(x, p, norm_mix, w_in, w_decay, b_decay, gla_norm, attn_sinks, w_branch_a,
           w_branch_b, w_out, norm_mlp, w_up, w_down, norm_ple, w_ple_gate, w_ple_proj,
           norm_final):
    b, s, d = x.shape
    depth = w_in.shape[0]
    assert depth == 1 and d == D_MODEL
    h = _layer(x.reshape(b * s, d), p[0].reshape(b * s, PLE_DIM), b, s,
               norm_mix[0], w_in[0], w_decay[0], b_decay[0], gla_norm[0], attn_sinks[0],
               w_branch_a[0], w_branch_b[0], w_out[0], norm_mlp[0], w_up[0], w_down[0],
               norm_ple[0], w_ple_gate[0], w_ple_proj[0],
               norm_final.reshape(1, D_MODEL))
    return h.reshape(b, s, d)
```

```python
import numpy as np
import jax
import jax.numpy as jnp
from jax import lax
from jax.experimental import pallas as pl
from jax.experimental.pallas import tpu as pltpu

F32 = jnp.float32
BF16 = jnp.bfloat16

D_MODEL = 2048
PLE_DIM = 256
GLA_HEADS = 4
GLA_DK = 128
GLA_DV = 256
GLA_RANK = 16
GLA_TAU = 16.0
GLA_CHUNK = 64
SWA_Q_HEADS = 16
SWA_KV_HEADS = 4
SWA_HEAD_DIM = 64
WINDOW = 128
D_FF = 4 * D_MODEL
EPS = 1e-6
NEG = -1e30

LANES = 128
VMEM_LIMIT = 56 * 1024 * 1024

COL_GA, COL_GB, COL_GV, COL_GR, COL_SQ, COL_GQ, COL_GK, COL_SK, COL_SV = (
    0, 2048, 4096, 5120, 6144, 7168, 7680, 8192, 8448)
PROJ_W = 8704
_SRC = dict(gq=0, gk=512, gv=1024, gr=2048, gz=3072, sq=3088, sk=4112, sv=4368,
            ga=4624, gb=6672, end=8720)


def _params(*sem):
    return pltpu.CompilerParams(dimension_semantics=sem, vmem_limit_bytes=VMEM_LIMIT)


def _rms_scale(x):
    return lax.rsqrt(jnp.mean(x * x, axis=-1, keepdims=True) + EPS)


INPROJ_TM = 256
INPROJ_TN = 512
NORM_ROWS = 128


def _resident(shape):
    return pl.BlockSpec(shape, lambda i: (0,) * len(shape), pipeline_mode=pl.Buffered(1))


def _inproj_kernel(x_ref, g_ref, w_ref, wz_ref, o_ref, oz_ref):
    x = x_ref[...]
    u = (x * _rms_scale(x) * g_ref[...]).astype(BF16)
    oz_ref[...] = jnp.dot(u, wz_ref[...], preferred_element_type=F32)
    for n0 in range(0, PROJ_W, INPROJ_TN):
        o_ref[:, n0:n0 + INPROJ_TN] = jnp.dot(
            u, w_ref[:, n0:n0 + INPROJ_TN], preferred_element_type=F32).astype(BF16)


def _inproj(x2, g, w_main, w_z):
    t = x2.shape[0]
    tm = min(INPROJ_TM, t)
    return pl.pallas_call(
        _inproj_kernel,
        grid=(t // tm,),
        in_specs=[
            pl.BlockSpec((tm, D_MODEL), lambda i: (i, 0)),
            _resident((1, D_MODEL)),
            _resident((D_MODEL, PROJ_W)),
            _resident((D_MODEL, LANES)),
        ],
        out_specs=[
            pl.BlockSpec((tm, PROJ_W), lambda i: (i, 0)),
            pl.BlockSpec((tm, LANES), lambda i: (i, 0)),
        ],
        out_shape=[
            jax.ShapeDtypeStruct((t, PROJ_W), BF16),
            jax.ShapeDtypeStruct((t, LANES), F32),
        ],
        compiler_params=_params("parallel"),
        name="inproj",
    )(x2, g, w_main, w_z)


GLA_LEVELS = tuple(GLA_CHUNK >> (l + 1) for l in range(6))
N_LEV = len(GLA_LEVELS)
GLA_GROUP = 8
LOG2E = float(np.log2(np.e))


def _gla_constants():
    c = GLA_CHUNK
    i = np.arange(c)[:, None]
    t = np.arange(c)[None, :]
    j = t
    blocks = [(t <= i), (t > i)]
    pmask = []
    for h in GLA_LEVELS:
        boundary = (i // (2 * h)) * (2 * h) + h - 1
        is_q = (i % (2 * h)) >= h
        blocks.append(np.where(is_q, (t > boundary) & (t <= i), (t > i) & (t <= boundary)))
        pmask.append((i // (2 * h) == j // (2 * h)) & is_q & ((j % (2 * h)) < h))
    pmask.append(i == j)
    w_all = np.concatenate(blocks, axis=0).astype(np.float32)
    w_all = np.concatenate([w_all, w_all], axis=1)
    return w_all, np.stack(pmask).astype(np.float32)


def _gla_kernel(q_ref, k_ref, v_ref, r_ref, z_ref, wd_ref, bd_ref, gain_ref,
                wall_ref, pm_ref, y_ref, la_ref):
    s = q_ref.shape[0]
    c = GLA_CHUNK
    g_n = min(GLA_GROUP, s // c)
    rows_n = g_n * c
    z = jnp.dot(z_ref[...].astype(BF16), wd_ref[...],
                preferred_element_type=F32) + bd_ref[...]
    la_ref[...] = ((jnp.minimum(z, 0.0) - jnp.log(1.0 + jnp.exp(-jnp.abs(z))))
                   * (LOG2E / GLA_TAU))
    scale = GLA_DK ** -0.5
    nt_b = (((2,), (2,)), ((0,), (0,)))

    def decays(gi):
        rows = slice(gi * rows_n, (gi + 1) * rows_n)
        la = la_ref[rows, :]
        hi = la.astype(BF16)
        lo = (la - hi.astype(F32)).astype(BF16)
        e = jnp.stack([
            jnp.exp2(jnp.dot(wall_ref[...],
                             jnp.concatenate([hi[g * c:(g + 1) * c], lo[g * c:(g + 1) * c]],
                                             axis=0),
                             preferred_element_type=F32))
            for g in range(g_n)])
        return rows, e

    def mix(rows, e, st):
        e_cum = e[:, 0:c]
        e_rev = e[:, c:2 * c]
        e_lev = e[:, 2 * c:].reshape(g_n, N_LEV, c, GLA_DK)
        q3 = (q_ref[rows, :].astype(F32) * scale).reshape(g_n, c, GLA_DK)
        k3 = k_ref[rows, :].astype(F32).reshape(g_n, c, GLA_DK)
        v3 = v_ref[rows, :].reshape(g_n, c, GLA_DV)
        qh = jnp.concatenate([(q3[:, None] * e_lev).astype(BF16),
                              q3.astype(BF16)[:, None]], axis=1)
        kh = jnp.concatenate([(k3[:, None] * e_lev).astype(BF16),
                              k3.astype(BF16)[:, None]], axis=1)
        a = lax.dot_general(qh.reshape(g_n * (N_LEV + 1), c, GLA_DK),
                            kh.reshape(g_n * (N_LEV + 1), c, GLA_DK), nt_b,
                            preferred_element_type=F32)
        a = jnp.sum(a.reshape(g_n, N_LEV + 1, c, c) * pm_ref[...][None], axis=1)
        kd = (k3 * e_rev).astype(BF16)
        states = []
        for g in range(g_n):
            states.append(st.astype(BF16))
            upd = lax.dot_general(v3[g], kd[g], (((0,), (0,)), ((), ())),
                                  preferred_element_type=F32)
            st = st * e_cum[g, c - 1:c, :] + upd
        return (rows, (q3 * e_cum).astype(BF16), jnp.stack(states), a.astype(BF16), v3), st

    def emit(rows, qe, states, a, v3):
        o = (lax.dot_general(qe, states, nt_b, preferred_element_type=F32)
             + lax.dot_general(a, v3, (((2,), (1,)), ((0,), (0,))),
                               preferred_element_type=F32)).reshape(rows_n, GLA_DV)
        o = o * _rms_scale(o) * gain_ref[...]
        r = r_ref[rows, :].astype(F32)
        y_ref[rows, :] = (o * (r * jax.nn.sigmoid(r))).astype(y_ref.dtype)

    n_groups = s // rows_n
    st = jnp.zeros((GLA_DV, GLA_DK), F32)
    pending = None
    dec = decays(0)
    for gi in range(n_groups):
        mixed, st = mix(*dec, st)
        if gi + 1 < n_groups:
            dec = decays(gi + 1)
        if pending is not None:
            emit(*pending)
        pending = mixed
    emit(*pending)


def _gla(proj3, gz3, wd, bd, gain):
    b, s, _ = proj3.shape
    w_all, pm = _gla_constants()
    const = lambda shape: pl.BlockSpec(shape, lambda bi, hi: (0,) * len(shape))
    return pl.pallas_call(
        _gla_kernel,
        grid=(b, GLA_HEADS),
        in_specs=[
            pl.BlockSpec((None, s, GLA_DK), lambda bi, hi: (bi, 0, COL_GQ // GLA_DK + hi)),
            pl.BlockSpec((None, s, GLA_DK), lambda bi, hi: (bi, 0, COL_GK // GLA_DK + hi)),
            pl.BlockSpec((None, s, GLA_DV), lambda bi, hi: (bi, 0, COL_GV // GLA_DV + hi)),
            pl.BlockSpec((None, s, GLA_DV), lambda bi, hi: (bi, 0, COL_GR // GLA_DV + hi)),
            pl.BlockSpec((None, s, LANES), lambda bi, hi: (bi, 0, 0)),
            pl.BlockSpec((None, LANES, GLA_DK), lambda bi, hi: (hi, 0, 0)),
            pl.BlockSpec((None, 1, GLA_DK), lambda bi, hi: (hi, 0, 0)),
            const((1, GLA_DV)),
            const(w_all.shape), const(pm.shape),
        ],
        out_specs=pl.BlockSpec((None, s, GLA_DV), lambda bi, hi: (bi, 0, hi)),
        out_shape=jax.ShapeDtypeStruct((b, s, GLA_HEADS * GLA_DV), BF16),
        scratch_shapes=[pltpu.VMEM((s, GLA_DK), F32)],
        compiler_params=_params("parallel", "parallel"),
        name="gla",
    )(proj3, proj3, proj3, proj3, gz3, wd, bd, gain,
      jnp.asarray(w_all, BF16), jnp.asarray(pm))


GROUP = SWA_Q_HEADS // SWA_KV_HEADS
SWA_KEYS = 2 * WINDOW


def _swa_bias():
    qi = np.arange(WINDOW)[None, :] + WINDOW
    kj = np.arange(SWA_KEYS)[:, None]
    dist = (qi - kj).astype(np.float64)
    allowed = (dist >= 0) & (dist < WINDOW)
    slopes = 2.0 ** (-8.0 * np.arange(1, SWA_Q_HEADS + 1) / SWA_Q_HEADS)
    bias = -slopes[:, None, None] * dist[None]
    later = np.where(allowed[None], bias, NEG)
    first = np.where((allowed & (kj >= WINDOW))[None], bias, NEG)
    return np.stack([first, later]).astype(np.float32)


SWA_BLOCKS = 4
KV_W = 2 * SWA_KV_HEADS * SWA_HEAD_DIM


def _swa_kernel(sink_ref, q_ref, kvp_ref, kvc_ref, bias_ref, o_ref):
    hd = SWA_HEAD_DIM
    w = WINDOW
    nb = q_ref.shape[0] // w
    kw = SWA_KV_HEADS * hd
    first = pl.program_id(1) == 0
    kv = jnp.concatenate([kvp_ref[...], kvc_ref[...]], axis=0)
    v_t = [kv[j * w:(j + 1) * w, kw:].T for j in range(nb + 1)]
    zeros = jnp.zeros((hd, w), BF16)

    def scores(r):
        q_t = (q_ref[r * w:(r + 1) * w, :] * (hd ** -0.5)).T
        bias_idx = jnp.where(first, 0, 1) if r == 0 else 1
        out = []
        for h in range(SWA_Q_HEADS):
            g = h // GROUP
            q_h = q_t[h * hd:(h + 1) * hd]
            rhs = jnp.concatenate([q_h, zeros] if g % 2 == 0 else [zeros, q_h], axis=0)
            out.append(jnp.dot(kv[r * w:(r + 2) * w, (g // 2) * LANES:(g // 2 + 1) * LANES],
                               rhs, preferred_element_type=F32) + bias_ref[bias_idx, h])
        return out

    def softmax(sc):
        probs, inv_dens = [], []
        for h in range(SWA_Q_HEADS):
            sink = sink_ref[h]
            m = jnp.maximum(jnp.max(sc[h], axis=0, keepdims=True), sink)
            e = jnp.exp(sc[h] - m)
            inv_dens.append(1.0 / (jnp.sum(e, axis=0, keepdims=True) + jnp.exp(sink - m)))
            probs.append(e.astype(BF16))
        return probs, inv_dens

    def values(r, probs, inv_dens):
        vt = jnp.concatenate([v_t[r], v_t[r + 1]], axis=1)
        outs = [jnp.dot(vt[(h // GROUP) * hd:(h // GROUP + 1) * hd], probs[h],
                        preferred_element_type=F32) * inv_dens[h]
                for h in range(SWA_Q_HEADS)]
        for pair in range(SWA_Q_HEADS // 2):
            tile = jnp.concatenate(outs[2 * pair:2 * pair + 2], axis=0)
            o_ref[r * w:(r + 1) * w, pair * LANES:(pair + 1) * LANES] = (
                tile.T.astype(o_ref.dtype))

    sc = scores(0)
    for r in range(nb):
        nxt = scores(r + 1) if r + 1 < nb else None
        values(r, *softmax(sc))
        sc = nxt


def _swa(proj3, sinks):
    b, s, _ = proj3.shape
    w = WINDOW
    nb = min(SWA_BLOCKS, s // w)
    qw = SWA_Q_HEADS * SWA_HEAD_DIM
    prev = lambda bi, n: (bi, jnp.maximum(nb * n - 1, 0), COL_SK // KV_W)
    bias = _swa_bias()
    return pl.pallas_call(
        _swa_kernel,
        grid=(b, s // (nb * w)),
        in_specs=[
            pl.BlockSpec(memory_space=pltpu.SMEM),
            pl.BlockSpec((None, nb * w, qw), lambda bi, n: (bi, n, COL_SQ // qw)),
            pl.BlockSpec((None, w, KV_W), prev),
            pl.BlockSpec((None, nb * w, KV_W), lambda bi, n: (bi, n, COL_SK // KV_W)),
            pl.BlockSpec(bias.shape, lambda bi, n: (0, 0, 0, 0)),
        ],
        out_specs=pl.BlockSpec((None, nb * w, qw), lambda bi, n: (bi, n, 0)),
        out_shape=jax.ShapeDtypeStruct((b, s, qw), BF16),
        compiler_params=_params("parallel", "parallel"),
        name="swa",
    )(sinks, proj3, proj3, proj3, jnp.asarray(bias))


MERGE_TM = 256


def _merge_kernel(ya_ref, yb_ref, ga_ref, gb_ref, x_ref, wa_ref, wb_ref, wo_ref, o_ref):
    a = jnp.dot(ya_ref[...], wa_ref[...], preferred_element_type=F32)
    bb = jnp.dot(yb_ref[...], wb_ref[...], preferred_element_type=F32)
    merged = (jax.nn.sigmoid(ga_ref[...].astype(F32)) * a
              + jax.nn.sigmoid(gb_ref[...].astype(F32)) * bb)
    o_ref[...] = x_ref[...] + jnp.dot(merged.astype(BF16), wo_ref[...],
                                      preferred_element_type=F32)


def _merge(ya, yb, proj, x2, wa, wb, wo):
    t = x2.shape[0]
    tm = MERGE_TM
    wa_w = GLA_HEADS * GLA_DV
    wb_w = SWA_Q_HEADS * SWA_HEAD_DIM
    return pl.pallas_call(
        _merge_kernel,
        grid=(t // tm,),
        in_specs=[
            pl.BlockSpec((tm, wa_w), lambda i: (i, 0)),
            pl.BlockSpec((tm, wb_w), lambda i: (i, 0)),
            pl.BlockSpec((tm, D_MODEL), lambda i: (i, COL_GA // D_MODEL)),
            pl.BlockSpec((tm, D_MODEL), lambda i: (i, COL_GB // D_MODEL)),
            pl.BlockSpec((tm, D_MODEL), lambda i: (i, 0)),
            _resident((wa_w, D_MODEL)),
            _resident((wb_w, D_MODEL)),
            _resident((D_MODEL, D_MODEL)),
        ],
        out_specs=pl.BlockSpec((tm, D_MODEL), lambda i: (i, 0)),
        out_shape=jax.ShapeDtypeStruct((t, D_MODEL), F32),
        compiler_params=_params("parallel"),
        name="merge",
    )(ya, yb, proj, proj, x2, wa, wb, wo)


MLP_TM = 1024
MLP_TF = 512


def _mlp_kernel(h_ref, g_ref, wu_ref, wd_ref, o_ref, hn_ref):
    @pl.when(pl.program_id(1) == 0)
    def _():
        def body(r, c):
            rows = pl.ds(pl.multiple_of(r * NORM_ROWS, NORM_ROWS), NORM_ROWS)
            x = h_ref[rows, :]
            hn_ref[rows, :] = (x * _rms_scale(x) * g_ref[...]).astype(BF16)
            o_ref[rows, :] = x
            return c
        lax.fori_loop(0, h_ref.shape[0] // NORM_ROWS, body, 0)

    up = jnp.dot(hn_ref[...], wu_ref[...], preferred_element_type=F32)
    act = jnp.square(jnp.maximum(up, 0.0)).astype(BF16)
    o_ref[...] += jnp.dot(act, wd_ref[...], preferred_element_type=F32)


def _mlp(h1, g, wu, wd):
    t = h1.shape[0]
    tm = min(MLP_TM, t)
    return pl.pallas_call(
        _mlp_kernel,
        grid=(t // tm, D_FF // MLP_TF),
        in_specs=[
            pl.BlockSpec((tm, D_MODEL), lambda i, j: (i, 0)),
            pl.BlockSpec((1, D_MODEL), lambda i, j: (0, 0)),
            pl.BlockSpec((D_MODEL, MLP_TF), lambda i, j: (0, j)),
            pl.BlockSpec((MLP_TF, D_MODEL), lambda i, j: (j, 0)),
        ],
        out_specs=pl.BlockSpec((tm, D_MODEL), lambda i, j: (i, 0)),
        out_shape=jax.ShapeDtypeStruct((t, D_MODEL), F32),
        scratch_shapes=[pltpu.VMEM((tm, D_MODEL), BF16)],
        compiler_params=_params("parallel", "arbitrary"),
        name="mlp",
    )(h1, g, wu, wd)


PLE_TM = 256


def _ple_kernel(h_ref, p_ref, g_ref, gf_ref, wg_ref, wp_ref, o_ref):
    h = h_ref[...]
    hp = (h * _rms_scale(h) * g_ref[...]).astype(BF16)
    gate = jax.nn.sigmoid(jnp.dot(hp, wg_ref[...], preferred_element_type=F32))
    emb = jnp.dot(p_ref[...].astype(BF16), wp_ref[...], preferred_element_type=F32)
    h = h + gate * emb
    o_ref[...] = h * _rms_scale(h) * gf_ref[...]


def _ple(h2, p2, g, gf, wg, wp):
    t = h2.shape[0]
    tm = PLE_TM
    return pl.pallas_call(
        _ple_kernel,
        grid=(t // tm,),
        in_specs=[
            pl.BlockSpec((tm, D_MODEL), lambda i: (i, 0)),
            pl.BlockSpec((tm, PLE_DIM), lambda i: (i, 0)),
            _resident((1, D_MODEL)),
            _resident((1, D_MODEL)),
            _resident((D_MODEL, D_MODEL)),
            _resident((PLE_DIM, D_MODEL)),
        ],
        out_specs=pl.BlockSpec((tm, D_MODEL), lambda i: (i, 0)),
        out_shape=jax.ShapeDtypeStruct((t, D_MODEL), F32),
        compiler_params=_params("parallel"),
        name="ple",
    )(h2, p2, g, gf, wg, wp)


def _layer(h2d, p2d, b, s, norm_mix, w_in, w_decay, b_decay, gla_norm, attn_sinks,
           w_branch_a, w_branch_b, w_out, norm_mlp, w_up, w_down, norm_ple,
           w_ple_gate, w_ple_proj, final_gain):
    src = _SRC
    seg = lambda a, z: w_in[:, src[a]:src[z]]
    w_main = jnp.concatenate(
        [seg("ga", "gb"), seg("gb", "end"), seg("gv", "gr"), seg("gr", "gz"),
         seg("sq", "sk"), seg("gq", "gk"), seg("gk", "gv"), seg("sk", "sv"),
         seg("sv", "ga")], axis=1).astype(BF16)
    w_z = jnp.pad(seg("gz", "sq"), ((0, 0), (0, LANES - GLA_RANK))).astype(BF16)
    proj, gz = _inproj(h2d, norm_mix.reshape(1, D_MODEL), w_main, w_z)
    proj3 = proj.reshape(b, s, PROJ_W)
    gz3 = gz.reshape(b, s, LANES)

    wd = jnp.pad(w_decay.reshape(GLA_RANK, GLA_HEADS, GLA_DK).transpose(1, 0, 2),
                 ((0, 0), (0, LANES - GLA_RANK), (0, 0))).astype(BF16)
    bd = b_decay.reshape(GLA_HEADS, 1, GLA_DK)
    ya = _gla(proj3, gz3, wd, bd, gla_norm.reshape(1, GLA_DV))
    yb = _swa(proj3, attn_sinks)

    t = b * s
    h1 = _merge(ya.reshape(t, -1), yb.reshape(t, -1), proj, h2d,
                w_branch_a.astype(BF16), w_branch_b.astype(BF16), w_out.astype(BF16))
    h2 = _mlp(h1, norm_mlp.reshape(1, D_MODEL), w_up.astype(BF16), w_down.astype(BF16))
    return _ple(h2, p2d, norm_ple.reshape(1, D_MODEL), final_gain,
                w_ple_gate.astype(BF16), w_ple_proj.astype(BF16))


def kernel(x, p, norm_mix, w_in, w_decay, b_decay, gla_norm, attn_sinks, w_branch_a,
           w_branch_b, w_out, norm_mlp, w_up, w_down, norm_ple, w_ple_gate, w_ple_proj,
           norm_final):
    b, s, d = x.shape
    depth = w_in.shape[0]
    assert depth == 1 and d == D_MODEL
    h = _layer(x.reshape(b * s, d), p[0].reshape(b * s, PLE_DIM), b, s,
               norm_mix[0], w_in[0], w_decay[0], b_decay[0], gla_norm[0], attn_sinks[0],
               w_branch_a[0], w_branch_b[0], w_out[0], norm_mlp[0], w_up[0], w_down[0],
               norm_ple[0], w_ple_gate[0], w_ple_proj[0],
               norm_final.reshape(1, D_MODEL))
    return h.reshape(b, s, d)
```

```python
import numpy as np
import jax
import jax.numpy as jnp
from jax import lax
from jax.experimental import pallas as pl
from jax.experimental.pallas import tpu as pltpu

F32 = jnp.float32
BF16 = jnp.bfloat16

D_MODEL = 2048
PLE_DIM = 256
GLA_HEADS = 4
GLA_DK = 128
GLA_DV = 256
GLA_RANK = 16
GLA_TAU = 16.0
GLA_CHUNK = 64
SWA_Q_HEADS = 16
SWA_KV_HEADS = 4
SWA_HEAD_DIM = 64
WINDOW = 128
D_FF = 4 * D_MODEL
EPS = 1e-6
NEG = -1e30

LANES = 128
VMEM_LIMIT = 60 * 1024 * 1024

COL_GA, COL_GB, COL_GV, COL_GR, COL_SQ, COL_GQ, COL_GK, COL_SK, COL_SV = (
    0, 2048, 4096, 5120, 6144, 7168, 7680, 8192, 8448)
PROJ_W = 8704
_SRC = dict(gq=0, gk=512, gv=1024, gr=2048, gz=3072, sq=3088, sk=4112, sv=4368,
            ga=4624, gb=6672, end=8720)


def _params(*sem):
    return pltpu.CompilerParams(dimension_semantics=sem, vmem_limit_bytes=VMEM_LIMIT)


def _rms_scale(x):
    return lax.rsqrt(jnp.mean(x * x, axis=-1, keepdims=True) + EPS)


INPROJ_TM = 256
INPROJ_TN = 512
NORM_ROWS = 128


def _resident(shape):
    return pl.BlockSpec(shape, lambda i: (0,) * len(shape), pipeline_mode=pl.Buffered(1))


def _inproj_segments():
    head = {k: _SRC[k] for k in ("gq", "gk", "gv", "gr")}
    tail = {k: _SRC[k] - _SRC["sq"] for k in ("sq", "sk", "sv", "ga", "gb")}
    dst = dict(ga=COL_GA, gb=COL_GB, gv=COL_GV, gr=COL_GR, sq=COL_SQ, gq=COL_GQ,
               gk=COL_GK, sk=COL_SK, sv=COL_SV)
    width = dict(ga=D_MODEL, gb=D_MODEL, gv=GLA_HEADS * GLA_DV, gr=GLA_HEADS * GLA_DV,
                 sq=SWA_Q_HEADS * SWA_HEAD_DIM, gq=GLA_HEADS * GLA_DK, gk=GLA_HEADS * GLA_DK,
                 sk=SWA_KV_HEADS * SWA_HEAD_DIM, sv=SWA_KV_HEADS * SWA_HEAD_DIM)
    segs = [(0, head[k], width[k], dst[k]) for k in head]
    segs += [(1, tail[k], width[k], dst[k]) for k in tail]
    return segs


def _inproj_kernel(x_ref, g_ref, wh_ref, wt_ref, wz_ref, o_ref, oz_ref):
    x = x_ref[...]
    u = (x * _rms_scale(x) * g_ref[...]).astype(BF16)
    oz_ref[...] = jnp.dot(u, wz_ref[...], preferred_element_type=F32)
    for piece, src, width, dst in _inproj_segments():
        w_ref = (wh_ref, wt_ref)[piece]
        step = min(width, INPROJ_TN)
        for off in range(0, width, step):
            o_ref[:, dst + off:dst + off + step] = jnp.dot(
                u, w_ref[:, src + off:src + off + step],
                preferred_element_type=F32).astype(BF16)


def _inproj(x2, g, w_head, w_tail, w_z):
    t = x2.shape[0]
    tm = min(INPROJ_TM, t)
    return pl.pallas_call(
        _inproj_kernel,
        grid=(t // tm,),
        in_specs=[
            pl.BlockSpec((tm, D_MODEL), lambda i: (i, 0)),
            _resident((1, D_MODEL)),
            _resident(w_head.shape),
            _resident(w_tail.shape),
            _resident((D_MODEL, LANES)),
        ],
        out_specs=[
            pl.BlockSpec((tm, PROJ_W), lambda i: (i, 0)),
            pl.BlockSpec((tm, LANES), lambda i: (i, 0)),
        ],
        out_shape=[
            jax.ShapeDtypeStruct((t, PROJ_W), BF16),
            jax.ShapeDtypeStruct((t, LANES), F32),
        ],
        compiler_params=_params("parallel"),
        name="inproj",
    )(x2, g, w_head, w_tail, w_z)


GLA_LEVELS = tuple(GLA_CHUNK >> (l + 1) for l in range(6))
N_LEV = len(GLA_LEVELS)
GLA_GROUP = 8
LOG2E = float(np.log2(np.e))


def _gla_constants():
    c = GLA_CHUNK
    i = np.arange(c)[:, None]
    t = np.arange(c)[None, :]
    j = t
    blocks = [(t <= i), (t > i)]
    pmask = []
    for h in GLA_LEVELS:
        boundary = (i // (2 * h)) * (2 * h) + h - 1
        is_q = (i % (2 * h)) >= h
        blocks.append(np.where(is_q, (t > boundary) & (t <= i), (t > i) & (t <= boundary)))
        pmask.append((i // (2 * h) == j // (2 * h)) & is_q & ((j % (2 * h)) < h))
    pmask.append(i == j)
    w_all = np.concatenate(blocks, axis=0).astype(np.float32)
    w_all = np.concatenate([w_all, w_all], axis=1)
    return w_all, np.stack(pmask).astype(np.float32)


def _gla_kernel(q_ref, k_ref, v_ref, r_ref, z_ref, wd_ref, bd_ref, gain_ref,
                wall_ref, pm_ref, y_ref, la_ref):
    s = q_ref.shape[0]
    c = GLA_CHUNK
    g_n = min(GLA_GROUP, s // c)
    rows_n = g_n * c
    z = jnp.dot(z_ref[...].astype(BF16), wd_ref[...],
                preferred_element_type=F32) + bd_ref[...]
    la_ref[...] = ((jnp.minimum(z, 0.0) - jnp.log(1.0 + jnp.exp(-jnp.abs(z))))
                   * (LOG2E / GLA_TAU))
    scale = GLA_DK ** -0.5
    nt_b = (((2,), (2,)), ((0,), (0,)))

    def decays(gi):
        rows = slice(gi * rows_n, (gi + 1) * rows_n)
        la = la_ref[rows, :]
        hi = la.astype(BF16)
        lo = (la - hi.astype(F32)).astype(BF16)
        e = jnp.stack([
            jnp.exp2(jnp.dot(wall_ref[...],
                             jnp.concatenate([hi[g * c:(g + 1) * c], lo[g * c:(g + 1) * c]],
                                             axis=0),
                             preferred_element_type=F32))
            for g in range(g_n)])
        return rows, e

    def mix(rows, e, st):
        e_cum = e[:, 0:c]
        e_rev = e[:, c:2 * c]
        e_lev = e[:, 2 * c:].reshape(g_n, N_LEV, c, GLA_DK)
        q3 = (q_ref[rows, :].astype(F32) * scale).reshape(g_n, c, GLA_DK)
        k3 = k_ref[rows, :].astype(F32).reshape(g_n, c, GLA_DK)
        v3 = v_ref[rows, :].reshape(g_n, c, GLA_DV)
        qh = jnp.concatenate([(q3[:, None] * e_lev).astype(BF16),
                              q3.astype(BF16)[:, None]], axis=1)
        kh = jnp.concatenate([(k3[:, None] * e_lev).astype(BF16),
                              k3.astype(BF16)[:, None]], axis=1)
        a = lax.dot_general(qh.reshape(g_n * (N_LEV + 1), c, GLA_DK),
                            kh.reshape(g_n * (N_LEV + 1), c, GLA_DK), nt_b,
                            preferred_element_type=F32)
        a = jnp.sum(a.reshape(g_n, N_LEV + 1, c, c) * pm_ref[...][None], axis=1)
        kd = (k3 * e_rev).astype(BF16)
        states = []
        for g in range(g_n):
            states.append(st.astype(BF16))
            upd = lax.dot_general(v3[g], kd[g], (((0,), (0,)), ((), ())),
                                  preferred_element_type=F32)
            st = st * e_cum[g, c - 1:c, :] + upd
        return (rows, (q3 * e_cum).astype(BF16), jnp.stack(states), a.astype(BF16), v3), st

    def emit(rows, qe, states, a, v3):
        o = (lax.dot_general(qe, states, nt_b, preferred_element_type=F32)
             + lax.dot_general(a, v3, (((2,), (1,)), ((0,), (0,))),
                               preferred_element_type=F32)).reshape(rows_n, GLA_DV)
        o = o * _rms_scale(o) * gain_ref[...]
        r = r_ref[rows, :].astype(F32)
        y_ref[rows, :] = (o * (r * jax.nn.sigmoid(r))).astype(y_ref.dtype)

    n_groups = s // rows_n
    st = jnp.zeros((GLA_DV, GLA_DK), F32)
    pending = None
    dec = decays(0)
    for gi in range(n_groups):
        mixed, st = mix(*dec, st)
        if gi + 1 < n_groups:
            dec = decays(gi + 1)
        if pending is not None:
            emit(*pending)
        pending = mixed
    emit(*pending)


def _gla(proj3, gz3, wd, bd, gain):
    b, s, _ = proj3.shape
    w_all, pm = _gla_constants()
    const = lambda shape: pl.BlockSpec(shape, lambda bi, hi: (0,) * len(shape))
    return pl.pallas_call(
        _gla_kernel,
        grid=(b, GLA_HEADS),
        in_specs=[
            pl.BlockSpec((None, s, GLA_DK), lambda bi, hi: (bi, 0, COL_GQ // GLA_DK + hi)),
            pl.BlockSpec((None, s, GLA_DK), lambda bi, hi: (bi, 0, COL_GK // GLA_DK + hi)),
            pl.BlockSpec((None, s, GLA_DV), lambda bi, hi: (bi, 0, COL_GV // GLA_DV + hi)),
            pl.BlockSpec((None, s, GLA_DV), lambda bi, hi: (bi, 0, COL_GR // GLA_DV + hi)),
            pl.BlockSpec((None, s, LANES), lambda bi, hi: (bi, 0, 0)),
            pl.BlockSpec((None, LANES, GLA_DK), lambda bi, hi: (hi, 0, 0)),
            pl.BlockSpec((None, 1, GLA_DK), lambda bi, hi: (hi, 0, 0)),
            const((1, GLA_DV)),
            const(w_all.shape), const(pm.shape),
        ],
        out_specs=pl.BlockSpec((None, s, GLA_DV), lambda bi, hi: (bi, 0, hi)),
        out_shape=jax.ShapeDtypeStruct((b, s, GLA_HEADS * GLA_DV), BF16),
        scratch_shapes=[pltpu.VMEM((s, GLA_DK), F32)],
        compiler_params=_params("parallel", "parallel"),
        name="gla",
    )(proj3, proj3, proj3, proj3, gz3, wd, bd, gain,
      jnp.asarray(w_all, BF16), jnp.asarray(pm))


GROUP = SWA_Q_HEADS // SWA_KV_HEADS
SWA_KEYS = 2 * WINDOW


def _swa_bias():
    qi = np.arange(WINDOW)[None, :] + WINDOW
    kj = np.arange(SWA_KEYS)[:, None]
    dist = (qi - kj).astype(np.float64)
    allowed = (dist >= 0) & (dist < WINDOW)
    slopes = 2.0 ** (-8.0 * np.arange(1, SWA_Q_HEADS + 1) / SWA_Q_HEADS)
    bias = -slopes[:, None, None] * dist[None]
    later = np.where(allowed[None], bias, NEG)
    first = np.where((allowed & (kj >= WINDOW))[None], bias, NEG)
    return np.stack([first, later]).astype(np.float32)


SWA_BLOCKS = 4
KV_W = 2 * SWA_KV_HEADS * SWA_HEAD_DIM


def _swa_kernel(sink_ref, q_ref, kvp_ref, kvc_ref, bias_ref, o_ref):
    hd = SWA_HEAD_DIM
    w = WINDOW
    nb = q_ref.shape[0] // w
    kw = SWA_KV_HEADS * hd
    first = pl.program_id(1) == 0
    kv = jnp.concatenate([kvp_ref[...], kvc_ref[...]], axis=0)
    v_t = [kv[j * w:(j + 1) * w, kw:].T for j in range(nb + 1)]
    zeros = jnp.zeros((hd, w), BF16)

    def scores(r):
        q_t = (q_ref[r * w:(r + 1) * w, :] * (hd ** -0.5)).T
        bias_idx = jnp.where(first, 0, 1) if r == 0 else 1
        out = []
        for h in range(SWA_Q_HEADS):
            g = h // GROUP
            q_h = q_t[h * hd:(h + 1) * hd]
            rhs = jnp.concatenate([q_h, zeros] if g % 2 == 0 else [zeros, q_h], axis=0)
            out.append(jnp.dot(kv[r * w:(r + 2) * w, (g // 2) * LANES:(g // 2 + 1) * LANES],
                               rhs, preferred_element_type=F32) + bias_ref[bias_idx, h])
        return out

    def softmax(sc):
        probs, inv_dens = [], []
        for h in range(SWA_Q_HEADS):
            sink = sink_ref[h]
            m = jnp.maximum(jnp.max(sc[h], axis=0, keepdims=True), sink)
            e = jnp.exp(sc[h] - m)
            inv_dens.append(1.0 / (jnp.sum(e, axis=0, keepdims=True) + jnp.exp(sink - m)))
            probs.append(e.astype(BF16))
        return probs, inv_dens

    def values(r, probs, inv_dens):
        vt = jnp.concatenate([v_t[r], v_t[r + 1]], axis=1)
        outs = [jnp.dot(vt[(h // GROUP) * hd:(h // GROUP + 1) * hd], probs[h],
                        preferred_element_type=F32) * inv_dens[h]
                for h in range(SWA_Q_HEADS)]
        for pair in range(SWA_Q_HEADS // 2):
            tile = jnp.concatenate(outs[2 * pair:2 * pair + 2], axis=0)
            o_ref[r * w:(r + 1) * w, pair * LANES:(pair + 1) * LANES] = (
                tile.T.astype(o_ref.dtype))

    sc = scores(0)
    for r in range(nb):
        nxt = scores(r + 1) if r + 1 < nb else None
        values(r, *softmax(sc))
        sc = nxt


def _swa(proj3, sinks):
    b, s, _ = proj3.shape
    w = WINDOW
    nb = min(SWA_BLOCKS, s // w)
    qw = SWA_Q_HEADS * SWA_HEAD_DIM
    prev = lambda bi, n: (bi, jnp.maximum(nb * n - 1, 0), COL_SK // KV_W)
    bias = _swa_bias()
    return pl.pallas_call(
        _swa_kernel,
        grid=(b, s // (nb * w)),
        in_specs=[
            pl.BlockSpec(memory_space=pltpu.SMEM),
            pl.BlockSpec((None, nb * w, qw), lambda bi, n: (bi, n, COL_SQ // qw)),
            pl.BlockSpec((None, w, KV_W), prev),
            pl.BlockSpec((None, nb * w, KV_W), lambda bi, n: (bi, n, COL_SK // KV_W)),
            pl.BlockSpec(bias.shape, lambda bi, n: (0, 0, 0, 0)),
        ],
        out_specs=pl.BlockSpec((None, nb * w, qw), lambda bi, n: (bi, n, 0)),
        out_shape=jax.ShapeDtypeStruct((b, s, qw), BF16),
        compiler_params=_params("parallel", "parallel"),
        name="swa",
    )(sinks, proj3, proj3, proj3, jnp.asarray(bias))


MERGE_TM = 256


def _merge_kernel(ya_ref, yb_ref, ga_ref, gb_ref, x_ref, wa_ref, wb_ref, wo_ref, o_ref):
    a = jnp.dot(ya_ref[...], wa_ref[...], preferred_element_type=F32)
    bb = jnp.dot(yb_ref[...], wb_ref[...], preferred_element_type=F32)
    merged = (jax.nn.sigmoid(ga_ref[...].astype(F32)) * a
              + jax.nn.sigmoid(gb_ref[...].astype(F32)) * bb)
    o_ref[...] = x_ref[...] + jnp.dot(merged.astype(BF16), wo_ref[...],
                                      preferred_element_type=F32)


def _merge(ya, yb, proj, x2, wa, wb, wo):
    t = x2.shape[0]
    tm = MERGE_TM
    wa_w = GLA_HEADS * GLA_DV
    wb_w = SWA_Q_HEADS * SWA_HEAD_DIM
    return pl.pallas_call(
        _merge_kernel,
        grid=(t // tm,),
        in_specs=[
            pl.BlockSpec((tm, wa_w), lambda i: (i, 0)),
            pl.BlockSpec((tm, wb_w), lambda i: (i, 0)),
            pl.BlockSpec((tm, D_MODEL), lambda i: (i, COL_GA // D_MODEL)),
            pl.BlockSpec((tm, D_MODEL), lambda i: (i, COL_GB // D_MODEL)),
            pl.BlockSpec((tm, D_MODEL), lambda i: (i, 0)),
            _resident((wa_w, D_MODEL)),
            _resident((wb_w, D_MODEL)),
            _resident((D_MODEL, D_MODEL)),
        ],
        out_specs=pl.BlockSpec((tm, D_MODEL), lambda i: (i, 0)),
        out_shape=jax.ShapeDtypeStruct((t, D_MODEL), F32),
        compiler_params=_params("parallel"),
        name="merge",
    )(ya, yb, proj, proj, x2, wa, wb, wo)


MLP_TM = 1024
MLP_TF = 1024


def _mlp_kernel(h_ref, g_ref, wu_ref, wd_ref, o_ref, hn_ref):
    @pl.when(pl.program_id(1) == 0)
    def _():
        def body(r, c):
            rows = pl.ds(pl.multiple_of(r * NORM_ROWS, NORM_ROWS), NORM_ROWS)
            x = h_ref[rows, :]
            hn_ref[rows, :] = (x * _rms_scale(x) * g_ref[...]).astype(BF16)
            o_ref[rows, :] = x
            return c
        lax.fori_loop(0, h_ref.shape[0] // NORM_ROWS, body, 0)

    up = jnp.dot(hn_ref[...], wu_ref[...], preferred_element_type=F32)
    act = jnp.square(jnp.maximum(up, 0.0)).astype(BF16)
    o_ref[...] += jnp.dot(act, wd_ref[...], preferred_element_type=F32)


def _mlp(h1, g, wu, wd):
    t = h1.shape[0]
    tm = min(MLP_TM, t)
    return pl.pallas_call(
        _mlp_kernel,
        grid=(t // tm, D_FF // MLP_TF),
        in_specs=[
            pl.BlockSpec((tm, D_MODEL), lambda i, j: (i, 0)),
            pl.BlockSpec((1, D_MODEL), lambda i, j: (0, 0)),
            pl.BlockSpec((D_MODEL, MLP_TF), lambda i, j: (0, j)),
            pl.BlockSpec((MLP_TF, D_MODEL), lambda i, j: (j, 0)),
        ],
        out_specs=pl.BlockSpec((tm, D_MODEL), lambda i, j: (i, 0)),
        out_shape=jax.ShapeDtypeStruct((t, D_MODEL), F32),
        scratch_shapes=[pltpu.VMEM((tm, D_MODEL), BF16)],
        compiler_params=_params("parallel", "arbitrary"),
        name="mlp",
    )(h1, g, wu, wd)


PLE_TM = 512
PLE_PARTS = 2


def _ple_kernel(h_ref, p_ref, g_ref, gf_ref, wg_ref, wp_ref, o_ref):
    rows = h_ref.shape[0] // PLE_PARTS
    parts = [slice(k * rows, (k + 1) * rows) for k in range(PLE_PARTS)]
    hs = [h_ref[r, :] for r in parts]
    hps = [(h * _rms_scale(h) * g_ref[...]).astype(BF16) for h in hs]
    gates, embs = [], []
    for r, hp in zip(parts, hps):
        gates.append(jnp.dot(hp, wg_ref[...], preferred_element_type=F32))
        embs.append(jnp.dot(p_ref[r, :].astype(BF16), wp_ref[...],
                            preferred_element_type=F32))
    for r, h, gate, emb in zip(parts, hs, gates, embs):
        h = h + jax.nn.sigmoid(gate) * emb
        o_ref[r, :] = h * _rms_scale(h) * gf_ref[...]


def _ple(h2, p2, g, gf, wg, wp):
    t = h2.shape[0]
    tm = PLE_TM
    return pl.pallas_call(
        _ple_kernel,
        grid=(t // tm,),
        in_specs=[
            pl.BlockSpec((tm, D_MODEL), lambda i: (i, 0)),
            pl.BlockSpec((tm, PLE_DIM), lambda i: (i, 0)),
            _resident((1, D_MODEL)),
            _resident((1, D_MODEL)),
            _resident((D_MODEL, D_MODEL)),
            _resident((PLE_DIM, D_MODEL)),
        ],
        out_specs=pl.BlockSpec((tm, D_MODEL), lambda i: (i, 0)),
        out_shape=jax.ShapeDtypeStruct((t, D_MODEL), F32),
        compiler_params=_params("parallel"),
        name="ple",
    )(h2, p2, g, gf, wg, wp)


def _layer(h2d, p2d, b, s, norm_mix, w_in, w_decay, b_decay, gla_norm, attn_sinks,
           w_branch_a, w_branch_b, w_out, norm_mlp, w_up, w_down, norm_ple,
           w_ple_gate, w_ple_proj, final_gain):
    w_head = w_in[:, :_SRC["gz"]].astype(BF16)
    w_tail = w_in[:, _SRC["sq"]:].astype(BF16)
    w_z = jnp.pad(w_in[:, _SRC["gz"]:_SRC["sq"]],
                  ((0, 0), (0, LANES - GLA_RANK))).astype(BF16)
    proj, gz = _inproj(h2d, norm_mix.reshape(1, D_MODEL), w_head, w_tail, w_z)
    proj3 = proj.reshape(b, s, PROJ_W)
    gz3 = gz.reshape(b, s, LANES)

    wd = jnp.pad(w_decay.reshape(GLA_RANK, GLA_HEADS, GLA_DK).transpose(1, 0, 2),
                 ((0, 0), (0, LANES - GLA_RANK), (0, 0))).astype(BF16)
    bd = b_decay.reshape(GLA_HEADS, 1, GLA_DK)
    ya = _gla(proj3, gz3, wd, bd, gla_norm.reshape(1, GLA_DV))
    yb = _swa(proj3, attn_sinks)

    t = b * s
    h1 = _merge(ya.reshape(t, -1), yb.reshape(t, -1), proj, h2d,
                w_branch_a.astype(BF16), w_branch_b.astype(BF16), w_out.astype(BF16))
    h2 = _mlp(h1, norm_mlp.reshape(1, D_MODEL), w_up.astype(BF16), w_down.astype(BF16))
    return _ple(h2, p2d, norm_ple.reshape(1, D_MODEL), final_gain,
                w_ple_gate.astype(BF16), w_ple_proj.astype(BF16))


def kernel(x, p, norm_mix, w_in, w_decay, b_decay, gla_norm, attn_sinks, w_branch_a,
           w_branch_b, w_out, norm_mlp, w_up, w_down, norm_ple, w_ple_gate, w_ple_proj,
           norm_final):
    b, s, d = x.shape
    depth = w_in.shape[0]
    assert depth == 1 and d == D_MODEL
    h = _layer(x.reshape(b * s, d), p[0].reshape(b * s, PLE_DIM), b, s,
               norm_mix[0], w_in[0], w_decay[0], b_decay[0], gla_norm[0], attn_sinks[0],
               w_branch_a[0], w_branch_b[0], w_out[0], norm_mlp[0], w_up[0], w_down[0],
               norm_ple[0], w_ple_gate[0], w_ple_proj[0],
               norm_final.reshape(1, D_MODEL))
    return h.reshape(b, s, d)
```

```python
import functools

import numpy as np
import jax
import jax.numpy as jnp
from jax import lax
from jax.experimental import pallas as pl
from jax.experimental.pallas import tpu as pltpu

F32 = jnp.float32
BF16 = jnp.bfloat16

D_MODEL = 2048
PLE_DIM = 256
GLA_HEADS = 4
GLA_DK = 128
GLA_DV = 256
GLA_RANK = 16
GLA_TAU = 16.0
GLA_CHUNK = 64
SWA_Q_HEADS = 16
SWA_KV_HEADS = 4
SWA_HEAD_DIM = 64
WINDOW = 128
D_FF = 4 * D_MODEL
EPS = 1e-6
NEG = -1e30

LANES = 128
VMEM_LIMIT = 60 * 1024 * 1024

COL_GA, COL_GB, COL_GV, COL_GR, COL_SQ, COL_GQ, COL_GK, COL_SK, COL_SV = (
    0, 2048, 4096, 5120, 6144, 7168, 7680, 8192, 8448)
PROJ_W = 8704
_SRC = dict(gq=0, gk=512, gv=1024, gr=2048, gz=3072, sq=3088, sk=4112, sv=4368,
            ga=4624, gb=6672, end=8720)


def _params(*sem):
    return pltpu.CompilerParams(dimension_semantics=sem, vmem_limit_bytes=VMEM_LIMIT)


def _rms_scale(x):
    return lax.rsqrt(jnp.mean(x * x, axis=-1, keepdims=True) + EPS)


def _resident(shape):
    return pl.BlockSpec(shape, lambda i: (0,) * len(shape), pipeline_mode=pl.Buffered(1))


INPROJ_TM = 256
INPROJ_TN = 512
NORM_ROWS = 128


def _inproj_segments():
    head = {k: _SRC[k] for k in ("gq", "gk", "gv", "gr")}
    tail = {k: _SRC[k] - _SRC["sq"] for k in ("sq", "sk", "sv", "ga", "gb")}
    dst = dict(ga=COL_GA, gb=COL_GB, gv=COL_GV, gr=COL_GR, sq=COL_SQ, gq=COL_GQ,
               gk=COL_GK, sk=COL_SK, sv=COL_SV)
    width = dict(ga=D_MODEL, gb=D_MODEL, gv=GLA_HEADS * GLA_DV, gr=GLA_HEADS * GLA_DV,
                 sq=SWA_Q_HEADS * SWA_HEAD_DIM, gq=GLA_HEADS * GLA_DK, gk=GLA_HEADS * GLA_DK,
                 sk=SWA_KV_HEADS * SWA_HEAD_DIM, sv=SWA_KV_HEADS * SWA_HEAD_DIM)
    segs = [(0, head[k], width[k], dst[k]) for k in head]
    segs += [(1, tail[k], width[k], dst[k]) for k in tail]
    return segs


def _inproj_kernel(x_ref, g_ref, wh_ref, wt_ref, wz_ref, o_ref, oz_ref):
    x = x_ref[...]
    u = (x * _rms_scale(x) * g_ref[...]).astype(BF16)
    oz_ref[...] = jnp.dot(u, wz_ref[...], preferred_element_type=F32)
    for piece, src, width, dst in _inproj_segments():
        w_ref = (wh_ref, wt_ref)[piece]
        step = min(width, INPROJ_TN)
        for off in range(0, width, step):
            o_ref[:, dst + off:dst + off + step] = jnp.dot(
                u, w_ref[:, src + off:src + off + step],
                preferred_element_type=F32).astype(BF16)


def _inproj(x2, g, w_head, w_tail, w_z):
    t = x2.shape[0]
    tm = min(INPROJ_TM, t)
    return pl.pallas_call(
        _inproj_kernel,
        grid=(t // tm,),
        in_specs=[
            pl.BlockSpec((tm, D_MODEL), lambda i: (i, 0)),
            _resident((1, D_MODEL)),
            _resident(w_head.shape),
            _resident(w_tail.shape),
            _resident((D_MODEL, LANES)),
        ],
        out_specs=[
            pl.BlockSpec((tm, PROJ_W), lambda i: (i, 0)),
            pl.BlockSpec((tm, LANES), lambda i: (i, 0)),
        ],
        out_shape=[
            jax.ShapeDtypeStruct((t, PROJ_W), BF16),
            jax.ShapeDtypeStruct((t, LANES), F32),
        ],
        compiler_params=_params("parallel"),
        name="inproj",
    )(x2, g, w_head, w_tail, w_z)


GLA_LEVELS = tuple(GLA_CHUNK >> (l + 1) for l in range(6))
N_LEV = len(GLA_LEVELS)
LOG2E = float(np.log2(np.e))
_NT_B = (((2,), (2,)), ((0,), (0,)))


def _gla_constants():
    c = GLA_CHUNK
    i = np.arange(c)[:, None]
    t = np.arange(c)[None, :]
    j = t
    blocks = [(t <= i), (t > i)]
    pmask = []
    for h in GLA_LEVELS:
        boundary = (i // (2 * h)) * (2 * h) + h - 1
        is_q = (i % (2 * h)) >= h
        blocks.append(np.where(is_q, (t > boundary) & (t <= i), (t > i) & (t <= boundary)))
        pmask.append((i // (2 * h) == j // (2 * h)) & is_q & ((j % (2 * h)) < h))
    pmask.append(i == j)
    w_all = np.concatenate(blocks, axis=0).astype(np.float32)
    w_all = np.concatenate([w_all, w_all], axis=1)
    return w_all, np.stack(pmask).astype(np.float32)


def _log2_decay(z):
    return (jnp.minimum(z, 0.0) - jnp.log(1.0 + jnp.exp(-jnp.abs(z)))) * (LOG2E / GLA_TAU)


def _gla_decays(la, wall_ref):
    c = GLA_CHUNK
    hi = la.astype(BF16)
    lo = (la - hi.astype(F32)).astype(BF16)
    return jnp.stack([
        jnp.exp2(jnp.dot(wall_ref[...],
                         jnp.concatenate([hi[g * c:(g + 1) * c], lo[g * c:(g + 1) * c]], axis=0),
                         preferred_element_type=F32))
        for g in range(la.shape[0] // c)])


def _gla_mix(e, q, k, v, st, pm_ref):
    c = GLA_CHUNK
    g_n = e.shape[0]
    e_cum = e[:, 0:c]
    e_rev = e[:, c:2 * c]
    e_lev = e[:, 2 * c:].reshape(g_n, N_LEV, c, GLA_DK)
    q3 = (q.astype(F32) * (GLA_DK ** -0.5)).reshape(g_n, c, GLA_DK)
    k3 = k.astype(F32).reshape(g_n, c, GLA_DK)
    v3 = v.reshape(g_n, c, GLA_DV)
    qh = jnp.concatenate([(q3[:, None] * e_lev).astype(BF16), q3.astype(BF16)[:, None]], axis=1)
    kh = jnp.concatenate([(k3[:, None] * e_lev).astype(BF16), k3.astype(BF16)[:, None]], axis=1)
    a = lax.dot_general(qh.reshape(g_n * (N_LEV + 1), c, GLA_DK),
                        kh.reshape(g_n * (N_LEV + 1), c, GLA_DK), _NT_B,
                        preferred_element_type=F32)
    a = jnp.sum(a.reshape(g_n, N_LEV + 1, c, c) * pm_ref[...][None], axis=1)
    kd = (k3 * e_rev).astype(BF16)
    states = []
    for g in range(g_n):
        states.append(st.astype(BF16))
        upd = lax.dot_general(v3[g], kd[g], (((0,), (0,)), ((), ())),
                              preferred_element_type=F32)
        st = st * e_cum[g, c - 1:c, :] + upd
    return ((q3 * e_cum).astype(BF16), jnp.stack(states), a.astype(BF16), v3), st


def _gla_emit(qe, states, a, v3, r, gain):
    o = (lax.dot_general(qe, states, _NT_B, preferred_element_type=F32)
         + lax.dot_general(a, v3, (((2,), (1,)), ((0,), (0,))), preferred_element_type=F32))
    o = o.reshape(-1, GLA_DV)
    o = o * _rms_scale(o) * gain
    r = r.astype(F32)
    return (o * (r * jax.nn.sigmoid(r))).astype(BF16)


GROUP = SWA_Q_HEADS // SWA_KV_HEADS
SWA_KEYS = 2 * WINDOW
KV_W = 2 * SWA_KV_HEADS * SWA_HEAD_DIM


def _swa_bias():
    qi = np.arange(WINDOW)[None, :] + WINDOW
    kj = np.arange(SWA_KEYS)[:, None]
    dist = (qi - kj).astype(np.float64)
    allowed = (dist >= 0) & (dist < WINDOW)
    slopes = 2.0 ** (-8.0 * np.arange(1, SWA_Q_HEADS + 1) / SWA_Q_HEADS)
    bias = -slopes[:, None, None] * dist[None]
    later = np.where(allowed[None], bias, NEG)
    first = np.where((allowed & (kj >= WINDOW))[None], bias, NEG)
    return np.stack([first, later]).astype(np.float32)


def _swa_scores(q, keys, bias_ref, bias_idx):
    hd = SWA_HEAD_DIM
    q_t = (q * (hd ** -0.5)).T
    zeros = jnp.zeros((hd, q.shape[0]), BF16)
    out = []
    for h in range(SWA_Q_HEADS):
        g = h // GROUP
        q_h = q_t[h * hd:(h + 1) * hd]
        rhs = jnp.concatenate([q_h, zeros] if g % 2 == 0 else [zeros, q_h], axis=0)
        out.append(jnp.dot(keys[:, (g // 2) * LANES:(g // 2 + 1) * LANES], rhs,
                           preferred_element_type=F32) + bias_ref[bias_idx, h])
    return out


def _swa_softmax(scores, sink_ref):
    probs, inv_dens = [], []
    for h in range(SWA_Q_HEADS):
        sink = sink_ref[h]
        m = jnp.maximum(jnp.max(scores[h], axis=0, keepdims=True), sink)
        e = jnp.exp(scores[h] - m)
        inv_dens.append(1.0 / (jnp.sum(e, axis=0, keepdims=True) + jnp.exp(sink - m)))
        probs.append(e.astype(BF16))
    return probs, inv_dens


def _swa_values(v_t, probs, inv_dens):
    hd = SWA_HEAD_DIM
    outs = [jnp.dot(v_t[(h // GROUP) * hd:(h // GROUP + 1) * hd], probs[h],
                    preferred_element_type=F32) * inv_dens[h]
            for h in range(SWA_Q_HEADS)]
    return [jnp.concatenate(outs[2 * p:2 * p + 2], axis=0).T.astype(BF16)
            for p in range(SWA_Q_HEADS // 2)]


MIX_TM = 256
MIX_W = COL_SK - COL_GV
OFF_GV, OFF_GR, OFF_SQ, OFF_GQ, OFF_GK = (c - COL_GV for c in
                                          (COL_GV, COL_GR, COL_SQ, COL_GQ, COL_GK))
YA_W = GLA_HEADS * GLA_DV
YB_W = SWA_Q_HEADS * SWA_HEAD_DIM


def _mixmerge_kernel(tiles_per_seq, sink_ref, mix_ref, kv_ref, gz_ref, gate_ref, x_ref,
                     wd_ref, bd_ref, gain_ref, wall_ref, pm_ref, bias_ref,
                     wa_ref, wb_ref, wo_ref, o_ref, y_ref, st_ref, kvp_ref):
    i = pl.program_id(0)
    last_tile = pl.num_programs(0) - 2
    first_of_seq = lax.rem(jnp.minimum(i, last_tile), tiles_per_seq) == 0
    slot = lax.rem(i, 2)
    w = WINDOW
    kw = SWA_KV_HEADS * SWA_HEAD_DIM

    @pl.when(i == 0)
    def _():
        y_ref[...] = jnp.zeros_like(y_ref)
        st_ref[...] = jnp.zeros_like(st_ref)
        kvp_ref[...] = jnp.zeros_like(kvp_ref)

    y_prev = y_ref[1 - slot]
    branch_a = jnp.dot(y_prev[:, :YA_W], wa_ref[...], preferred_element_type=F32)

    z = jnp.dot(gz_ref[...].astype(BF16), wd_ref[...], preferred_element_type=F32) + bd_ref[...]
    la = _log2_decay(z)
    decays = [_gla_decays(la[:, h * GLA_DK:(h + 1) * GLA_DK], wall_ref)
              for h in range(GLA_HEADS)]

    kv = jnp.concatenate([kvp_ref[...], kv_ref[...]], axis=0)
    kvp_ref[...] = kv_ref[MIX_TM - w:, :]
    v_t = [kv[j * w:(j + 1) * w, kw:].T for j in range(3)]
    q_blk = lambda r: mix_ref[r * w:(r + 1) * w, OFF_SQ:OFF_SQ + YB_W]
    sc0 = _swa_scores(q_blk(0), kv[0:2 * w], bias_ref, jnp.where(first_of_seq, 0, 1))

    branch_b = jnp.dot(y_prev[:, YA_W:], wb_ref[...], preferred_element_type=F32)

    mixed = []
    for h in range(GLA_HEADS):
        st = jnp.where(first_of_seq, 0.0, st_ref[h])
        m_h, st = _gla_mix(decays[h],
                           mix_ref[:, OFF_GQ + h * GLA_DK:OFF_GQ + (h + 1) * GLA_DK],
                           mix_ref[:, OFF_GK + h * GLA_DK:OFF_GK + (h + 1) * GLA_DK],
                           mix_ref[:, OFF_GV + h * GLA_DV:OFF_GV + (h + 1) * GLA_DV],
                           st, pm_ref)
        st_ref[h] = st
        mixed.append(m_h)

    sc1 = _swa_scores(q_blk(1), kv[w:3 * w], bias_ref, 1)
    yb0 = _swa_values(jnp.concatenate([v_t[0], v_t[1]], axis=1), *_swa_softmax(sc0, sink_ref))

    merged = (jax.nn.sigmoid(gate_ref[:, :D_MODEL].astype(F32)) * branch_a
              + jax.nn.sigmoid(gate_ref[:, D_MODEL:].astype(F32)) * branch_b).astype(BF16)
    half = D_MODEL // 2
    o_ref[:, :half] = x_ref[:, :half] + jnp.dot(merged, wo_ref[:, :half],
                                                preferred_element_type=F32)

    for h in range(GLA_HEADS):
        y_ref[slot, :, h * GLA_DV:(h + 1) * GLA_DV] = _gla_emit(
            *mixed[h], mix_ref[:, OFF_GR + h * GLA_DV:OFF_GR + (h + 1) * GLA_DV],
            gain_ref[...])

    yb1 = _swa_values(jnp.concatenate([v_t[1], v_t[2]], axis=1), *_swa_softmax(sc1, sink_ref))
    for r, tiles in enumerate((yb0, yb1)):
        for p, tile in enumerate(tiles):
            y_ref[slot, r * w:(r + 1) * w, YA_W + p * LANES:YA_W + (p + 1) * LANES] = tile

    o_ref[:, half:] = x_ref[:, half:] + jnp.dot(merged, wo_ref[:, half:],
                                                preferred_element_type=F32)


def _mixmerge(proj, gz, x2, seq_len, wd, bd, gain, sinks, wa, wb, wo):
    t = x2.shape[0]
    tm = MIX_TM
    n = t // tm
    w_all, pm = _gla_constants()
    bias = _swa_bias()
    cur = lambda col: (lambda i: (jnp.minimum(i, n - 1), col))
    lag = lambda i: (jnp.maximum(i - 1, 0), 0)
    return pl.pallas_call(
        functools.partial(_mixmerge_kernel, seq_len // tm),
        grid=(n + 1,),
        in_specs=[
            pl.BlockSpec(memory_space=pltpu.SMEM),
            pl.BlockSpec((tm, MIX_W), cur(COL_GV // MIX_W)),
            pl.BlockSpec((tm, KV_W), cur(COL_SK // KV_W)),
            pl.BlockSpec((tm, LANES), cur(0)),
            pl.BlockSpec((tm, 2 * D_MODEL), lag),
            pl.BlockSpec((tm, D_MODEL), lag),
            _resident(wd.shape), _resident(bd.shape), _resident(gain.shape),
            _resident(w_all.shape), _resident(pm.shape), _resident(bias.shape),
            _resident(wa.shape), _resident(wb.shape), _resident(wo.shape),
        ],
        out_specs=pl.BlockSpec((tm, D_MODEL), lag),
        out_shape=jax.ShapeDtypeStruct((t, D_MODEL), F32),
        scratch_shapes=[pltpu.VMEM((2, tm, YA_W + YB_W), BF16),
                        pltpu.VMEM((GLA_HEADS, GLA_DV, GLA_DK), F32),
                        pltpu.VMEM((WINDOW, KV_W), BF16)],
        compiler_params=_params("arbitrary"),
        name="mixmerge",
    )(sinks, proj, proj, gz, proj, x2, wd, bd, gain,
      jnp.asarray(w_all, BF16), jnp.asarray(pm), jnp.asarray(bias), wa, wb, wo)


MLP_TM = 1024
MLP_TF = 1024


def _mlp_kernel(h_ref, g_ref, wu_ref, wd_ref, o_ref, hn_ref):
    @pl.when(pl.program_id(1) == 0)
    def _():
        def body(r, c):
            rows = pl.ds(pl.multiple_of(r * NORM_ROWS, NORM_ROWS), NORM_ROWS)
            x = h_ref[rows, :]
            hn_ref[rows, :] = (x * _rms_scale(x) * g_ref[...]).astype(BF16)
            o_ref[rows, :] = x
            return c
        lax.fori_loop(0, h_ref.shape[0] // NORM_ROWS, body, 0)

    up = jnp.dot(hn_ref[...], wu_ref[...], preferred_element_type=F32)
    act = jnp.square(jnp.maximum(up, 0.0)).astype(BF16)
    o_ref[...] += jnp.dot(act, wd_ref[...], preferred_element_type=F32)


def _mlp(h1, g, wu, wd):
    t = h1.shape[0]
    tm = min(MLP_TM, t)
    return pl.pallas_call(
        _mlp_kernel,
        grid=(t // tm, D_FF // MLP_TF),
        in_specs=[
            pl.BlockSpec((tm, D_MODEL), lambda i, j: (i, 0)),
            pl.BlockSpec((1, D_MODEL), lambda i, j: (0, 0)),
            pl.BlockSpec((D_MODEL, MLP_TF), lambda i, j: (0, j)),
            pl.BlockSpec((MLP_TF, D_MODEL), lambda i, j: (j, 0)),
        ],
        out_specs=pl.BlockSpec((tm, D_MODEL), lambda i, j: (i, 0)),
        out_shape=jax.ShapeDtypeStruct((t, D_MODEL), F32),
        scratch_shapes=[pltpu.VMEM((tm, D_MODEL), BF16)],
        compiler_params=_params("parallel", "arbitrary"),
        name="mlp",
    )(h1, g, wu, wd)


PLE_TM = 512
PLE_PARTS = 2


def _ple_kernel(h_ref, p_ref, g_ref, gf_ref, wg_ref, wp_ref, o_ref):
    rows = h_ref.shape[0] // PLE_PARTS
    parts = [slice(k * rows, (k + 1) * rows) for k in range(PLE_PARTS)]
    hs = [h_ref[r, :] for r in parts]
    hps = [(h * _rms_scale(h) * g_ref[...]).astype(BF16) for h in hs]
    gates, embs = [], []
    for r, hp in zip(parts, hps):
        gates.append(jnp.dot(hp, wg_ref[...], preferred_element_type=F32))
        embs.append(jnp.dot(p_ref[r, :].astype(BF16), wp_ref[...],
                            preferred_element_type=F32))
    for r, h, gate, emb in zip(parts, hs, gates, embs):
        h = h + jax.nn.sigmoid(gate) * emb
        o_ref[r, :] = h * _rms_scale(h) * gf_ref[...]


def _ple(h2, p2, g, gf, wg, wp):
    t = h2.shape[0]
    tm = PLE_TM
    return pl.pallas_call(
        _ple_kernel,
        grid=(t // tm,),
        in_specs=[
            pl.BlockSpec((tm, D_MODEL), lambda i: (i, 0)),
            pl.BlockSpec((tm, PLE_DIM), lambda i: (i, 0)),
            _resident((1, D_MODEL)),
            _resident((1, D_MODEL)),
            _resident((D_MODEL, D_MODEL)),
            _resident((PLE_DIM, D_MODEL)),
        ],
        out_specs=pl.BlockSpec((tm, D_MODEL), lambda i: (i, 0)),
        out_shape=jax.ShapeDtypeStruct((t, D_MODEL), F32),
        compiler_params=_params("parallel"),
        name="ple",
    )(h2, p2, g, gf, wg, wp)


def _layer(h2d, p2d, s, norm_mix, w_in, w_decay, b_decay, gla_norm, attn_sinks,
           w_branch_a, w_branch_b, w_out, norm_mlp, w_up, w_down, norm_ple,
           w_ple_gate, w_ple_proj, final_gain):
    w_head = w_in[:, :_SRC["gz"]].astype(BF16)
    w_tail = w_in[:, _SRC["sq"]:].astype(BF16)
    w_z = jnp.pad(w_in[:, _SRC["gz"]:_SRC["sq"]],
                  ((0, 0), (0, LANES - GLA_RANK))).astype(BF16)
    proj, gz = _inproj(h2d, norm_mix.reshape(1, D_MODEL), w_head, w_tail, w_z)
    wd = jnp.pad(w_decay, ((0, LANES - GLA_RANK), (0, 0))).astype(BF16)
    h1 = _mixmerge(proj, gz, h2d, s, wd, b_decay.reshape(1, -1), gla_norm.reshape(1, GLA_DV),
                   attn_sinks, w_branch_a.astype(BF16), w_branch_b.astype(BF16),
                   w_out.astype(BF16))
    h2 = _mlp(h1, norm_mlp.reshape(1, D_MODEL), w_up.astype(BF16), w_down.astype(BF16))
    return _ple(h2, p2d, norm_ple.reshape(1, D_MODEL), final_gain,
                w_ple_gate.astype(BF16), w_ple_proj.astype(BF16))


def kernel(x, p, norm_mix, w_in, w_decay, b_decay, gla_norm, attn_sinks, w_branch_a,
           w_branch_b, w_out, norm_mlp, w_up, w_down, norm_ple, w_ple_gate, w_ple_proj,
           norm_final):
    b, s, d = x.shape
    depth = w_in.shape[0]
    assert depth == 1 and d == D_MODEL and s % MIX_TM == 0
    h = _layer(x.reshape(b * s, d), p[0].reshape(b * s, PLE_DIM), s,
               norm_mix[0], w_in[0], w_decay[0], b_decay[0], gla_norm[0], attn_sinks[0],
               w_branch_a[0], w_branch_b[0], w_out[0], norm_mlp[0], w_up[0], w_down[0],
               norm_ple[0], w_ple_gate[0], w_ple_proj[0],
               norm_final.reshape(1, D_MODEL))
    return h.reshape(b, s, d)
```

```python
import functools

import numpy as np
import jax
import jax.numpy as jnp
from jax import lax
from jax.experimental import pallas as pl
from jax.experimental.pallas import tpu as pltpu

F32 = jnp.float32
BF16 = jnp.bfloat16

D_MODEL = 2048
PLE_DIM = 256
GLA_HEADS = 4
GLA_DK = 128
GLA_DV = 256
GLA_RANK = 16
GLA_TAU = 16.0
GLA_CHUNK = 64
SWA_Q_HEADS = 16
SWA_KV_HEADS = 4
SWA_HEAD_DIM = 64
WINDOW = 128
D_FF = 4 * D_MODEL
EPS = 1e-6
NEG = -1e30

LANES = 128
VMEM_LIMIT = 60 * 1024 * 1024

COL_GA, COL_GB, COL_GV, COL_GR, COL_SQ, COL_GQ, COL_GK, COL_SK, COL_SV = (
    0, 2048, 4096, 5120, 6144, 7168, 7680, 8192, 8448)
PROJ_W = 8704
_SRC = dict(gq=0, gk=512, gv=1024, gr=2048, gz=3072, sq=3088, sk=4112, sv=4368,
            ga=4624, gb=6672, end=8720)


def _params(*sem):
    return pltpu.CompilerParams(dimension_semantics=sem, vmem_limit_bytes=VMEM_LIMIT)


def _rms_scale(x):
    return lax.rsqrt(jnp.mean(x * x, axis=-1, keepdims=True) + EPS)


def _resident(shape):
    return pl.BlockSpec(shape, lambda i: (0,) * len(shape), pipeline_mode=pl.Buffered(1))


INPROJ_TM = 256
INPROJ_TN = 512
NORM_ROWS = 128


def _inproj_segments():
    head = {k: _SRC[k] for k in ("gq", "gk", "gv", "gr")}
    tail = {k: _SRC[k] - _SRC["sq"] for k in ("sq", "sk", "sv", "ga", "gb")}
    dst = dict(ga=COL_GA, gb=COL_GB, gv=COL_GV, gr=COL_GR, sq=COL_SQ, gq=COL_GQ,
               gk=COL_GK, sk=COL_SK, sv=COL_SV)
    width = dict(ga=D_MODEL, gb=D_MODEL, gv=GLA_HEADS * GLA_DV, gr=GLA_HEADS * GLA_DV,
                 sq=SWA_Q_HEADS * SWA_HEAD_DIM, gq=GLA_HEADS * GLA_DK, gk=GLA_HEADS * GLA_DK,
                 sk=SWA_KV_HEADS * SWA_HEAD_DIM, sv=SWA_KV_HEADS * SWA_HEAD_DIM)
    segs = [(0, head[k], width[k], dst[k]) for k in head]
    segs += [(1, tail[k], width[k], dst[k]) for k in tail]
    return segs


def _inproj_kernel(x_ref, g_ref, wh_ref, wt_ref, wz_ref, o_ref, oz_ref):
    x = x_ref[...]
    u = (x * _rms_scale(x) * g_ref[...]).astype(BF16)
    oz_ref[...] = jnp.dot(u, wz_ref[...], preferred_element_type=F32)
    for piece, src, width, dst in _inproj_segments():
        w_ref = (wh_ref, wt_ref)[piece]
        step = min(width, INPROJ_TN)
        for off in range(0, width, step):
            o_ref[:, dst + off:dst + off + step] = jnp.dot(
                u, w_ref[:, src + off:src + off + step],
                preferred_element_type=F32).astype(BF16)


def _inproj(x2, g, w_head, w_tail, w_z):
    t = x2.shape[0]
    tm = min(INPROJ_TM, t)
    return pl.pallas_call(
        _inproj_kernel,
        grid=(t // tm,),
        in_specs=[
            pl.BlockSpec((tm, D_MODEL), lambda i: (i, 0)),
            _resident((1, D_MODEL)),
            _resident(w_head.shape),
            _resident(w_tail.shape),
            _resident((D_MODEL, LANES)),
        ],
        out_specs=[
            pl.BlockSpec((tm, PROJ_W), lambda i: (i, 0)),
            pl.BlockSpec((tm, LANES), lambda i: (i, 0)),
        ],
        out_shape=[
            jax.ShapeDtypeStruct((t, PROJ_W), BF16),
            jax.ShapeDtypeStruct((t, LANES), F32),
        ],
        compiler_params=_params("parallel"),
        name="inproj",
    )(x2, g, w_head, w_tail, w_z)


GLA_LEVELS = tuple(GLA_CHUNK >> (l + 1) for l in range(6))
N_LEV = len(GLA_LEVELS)
LOG2E = float(np.log2(np.e))
_NT_B = (((2,), (2,)), ((0,), (0,)))


def _gla_constants():
    c = GLA_CHUNK
    i = np.arange(c)[:, None]
    t = np.arange(c)[None, :]
    j = t
    blocks = [(t <= i), (t > i)]
    pmask = []
    for h in GLA_LEVELS:
        boundary = (i // (2 * h)) * (2 * h) + h - 1
        is_q = (i % (2 * h)) >= h
        blocks.append(np.where(is_q, (t > boundary) & (t <= i), (t > i) & (t <= boundary)))
        pmask.append((i // (2 * h) == j // (2 * h)) & is_q & ((j % (2 * h)) < h))
    pmask.append(i == j)
    w_all = np.concatenate(blocks, axis=0).astype(np.float32)
    w_all = np.concatenate([w_all, w_all], axis=1)
    return w_all, np.stack(pmask).astype(np.float32)


def _log2_decay(z):
    return (jnp.minimum(z, 0.0) - jnp.log(1.0 + jnp.exp(-jnp.abs(z)))) * (LOG2E / GLA_TAU)


def _gla_decays(la, wall_ref):
    c = GLA_CHUNK
    hi = la.astype(BF16)
    lo = (la - hi.astype(F32)).astype(BF16)
    return jnp.stack([
        jnp.exp2(jnp.dot(wall_ref[...],
                         jnp.concatenate([hi[g * c:(g + 1) * c], lo[g * c:(g + 1) * c]], axis=0),
                         preferred_element_type=F32))
        for g in range(la.shape[0] // c)])


def _gla_mix(e, q, k, v, st, pm_ref, between=lambda: None):
    c = GLA_CHUNK
    g_n = e.shape[0]
    e_cum = e[:, 0:c]
    e_rev = e[:, c:2 * c]
    e_lev = e[:, 2 * c:].reshape(g_n, N_LEV, c, GLA_DK)
    q3 = (q.astype(F32) * (GLA_DK ** -0.5)).reshape(g_n, c, GLA_DK)
    k3 = k.astype(F32).reshape(g_n, c, GLA_DK)
    v3 = v.reshape(g_n, c, GLA_DV)
    qh = jnp.concatenate([(q3[:, None] * e_lev).astype(BF16), q3.astype(BF16)[:, None]], axis=1)
    kh = jnp.concatenate([(k3[:, None] * e_lev).astype(BF16), k3.astype(BF16)[:, None]], axis=1)
    a = lax.dot_general(qh.reshape(g_n * (N_LEV + 1), c, GLA_DK),
                        kh.reshape(g_n * (N_LEV + 1), c, GLA_DK), _NT_B,
                        preferred_element_type=F32)
    between()
    a = jnp.sum(a.reshape(g_n, N_LEV + 1, c, c) * pm_ref[...][None], axis=1)
    kd = (k3 * e_rev).astype(BF16)
    states = []
    for g in range(g_n):
        states.append(st.astype(BF16))
        upd = lax.dot_general(v3[g], kd[g], (((0,), (0,)), ((), ())),
                              preferred_element_type=F32)
        st = st * e_cum[g, c - 1:c, :] + upd
    return ((q3 * e_cum).astype(BF16), jnp.stack(states), a.astype(BF16), v3), st


def _gla_emit(qe, states, a, v3, r, gain):
    o = (lax.dot_general(qe, states, _NT_B, preferred_element_type=F32)
         + lax.dot_general(a, v3, (((2,), (1,)), ((0,), (0,))), preferred_element_type=F32))
    o = o.reshape(-1, GLA_DV)
    o = o * _rms_scale(o) * gain
    r = r.astype(F32)
    return (o * (r * jax.nn.sigmoid(r))).astype(BF16)


GROUP = SWA_Q_HEADS // SWA_KV_HEADS
SWA_KEYS = 2 * WINDOW
KV_W = 2 * SWA_KV_HEADS * SWA_HEAD_DIM


def _swa_bias():
    qi = np.arange(WINDOW)[None, :] + WINDOW
    kj = np.arange(SWA_KEYS)[:, None]
    dist = (qi - kj).astype(np.float64)
    allowed = (dist >= 0) & (dist < WINDOW)
    slopes = 2.0 ** (-8.0 * np.arange(1, SWA_Q_HEADS + 1) / SWA_Q_HEADS)
    bias = -slopes[:, None, None] * dist[None]
    later = np.where(allowed[None], bias, NEG)
    first = np.where((allowed & (kj >= WINDOW))[None], bias, NEG)
    return np.stack([first, later]).astype(np.float32)


def _swa_scores(q, keys, bias_ref, bias_idx):
    hd = SWA_HEAD_DIM
    q_t = (q * (hd ** -0.5)).T
    zeros = jnp.zeros((hd, q.shape[0]), BF16)
    out = []
    for h in range(SWA_Q_HEADS):
        g = h // GROUP
        q_h = q_t[h * hd:(h + 1) * hd]
        rhs = jnp.concatenate([q_h, zeros] if g % 2 == 0 else [zeros, q_h], axis=0)
        out.append(jnp.dot(keys[:, (g // 2) * LANES:(g // 2 + 1) * LANES], rhs,
                           preferred_element_type=F32) + bias_ref[bias_idx, h])
    return out


def _swa_softmax(scores, sink_ref):
    probs, inv_dens = [], []
    for h in range(SWA_Q_HEADS):
        sink = sink_ref[h]
        m = jnp.maximum(jnp.max(scores[h], axis=0, keepdims=True), sink)
        e = jnp.exp(scores[h] - m)
        inv_dens.append(1.0 / (jnp.sum(e, axis=0, keepdims=True) + jnp.exp(sink - m)))
        probs.append(e.astype(BF16))
    return probs, inv_dens


def _swa_values(v_t, probs, inv_dens):
    hd = SWA_HEAD_DIM
    outs = [jnp.dot(v_t[(h // GROUP) * hd:(h // GROUP + 1) * hd], probs[h],
                    preferred_element_type=F32) * inv_dens[h]
            for h in range(SWA_Q_HEADS)]
    return [jnp.concatenate(outs[2 * p:2 * p + 2], axis=0).T.astype(BF16)
            for p in range(SWA_Q_HEADS // 2)]


MIX_TM = 256
MERGE_CHUNK = 256
MIX_W = COL_SK - COL_GV
OFF_GV, OFF_GR, OFF_SQ, OFF_GQ, OFF_GK = (c - COL_GV for c in
                                          (COL_GV, COL_GR, COL_SQ, COL_GQ, COL_GK))
YA_W = GLA_HEADS * GLA_DV
YB_W = SWA_Q_HEADS * SWA_HEAD_DIM


def _mixmerge_kernel(tiles_per_seq, sink_ref, mix_ref, kv_ref, gz_ref, gate_ref, x_ref,
                     wd_ref, bd_ref, gain_ref, wall_ref, pm_ref, bias_ref,
                     wa_ref, wb_ref, wo_ref, o_ref, y_ref, st_ref, kvp_ref):
    i = pl.program_id(0)
    last_tile = pl.num_programs(0) - 2
    first_of_seq = lax.rem(jnp.minimum(i, last_tile), tiles_per_seq) == 0
    slot = lax.rem(i, 2)
    w = WINDOW
    kw = SWA_KV_HEADS * SWA_HEAD_DIM

    @pl.when(i == 0)
    def _():
        y_ref[...] = jnp.zeros_like(y_ref)
        st_ref[...] = jnp.zeros_like(st_ref)
        kvp_ref[...] = jnp.zeros_like(kvp_ref)

    y_prev = y_ref[1 - slot]
    ck = MERGE_CHUNK
    nc = D_MODEL // ck
    col = lambda c: slice(c * ck, (c + 1) * ck)
    branch_a = {}
    merged = {}

    def chunk_a(c):
        branch_a[c] = jnp.dot(y_prev[:, :YA_W], wa_ref[:, col(c)], preferred_element_type=F32)

    def chunk_b(c):
        bb = jnp.dot(y_prev[:, YA_W:], wb_ref[:, col(c)], preferred_element_type=F32)
        merged[c] = (jax.nn.sigmoid(gate_ref[:, col(c)].astype(F32)) * branch_a[c]
                     + jax.nn.sigmoid(gate_ref[:, col(nc + c)].astype(F32)) * bb).astype(BF16)

    def chunk_o(c):
        lhs = jnp.concatenate([merged[k] for k in range(nc)], axis=1)
        o_ref[:, col(c)] = x_ref[:, col(c)] + jnp.dot(lhs, wo_ref[:, col(c)],
                                                      preferred_element_type=F32)

    queue = ([functools.partial(chunk_a, c) for c in range(nc)]
             + [functools.partial(chunk_b, c) for c in range(nc)]
             + [functools.partial(chunk_o, c) for c in range(nc)])

    def issue(k=1):
        for _ in range(k):
            if queue:
                queue.pop(0)()

    issue()
    z = jnp.dot(gz_ref[...].astype(BF16), wd_ref[...], preferred_element_type=F32) + bd_ref[...]
    la = _log2_decay(z)
    kv = jnp.concatenate([kvp_ref[...], kv_ref[...]], axis=0)
    kvp_ref[...] = kv_ref[MIX_TM - w:, :]
    v_t = [kv[j * w:(j + 1) * w, kw:].T for j in range(3)]
    q_blk = lambda r: mix_ref[r * w:(r + 1) * w, OFF_SQ:OFF_SQ + YB_W]
    issue()

    decays = []
    for h in range(GLA_HEADS):
        decays.append(_gla_decays(la[:, h * GLA_DK:(h + 1) * GLA_DK], wall_ref))
        issue()
    sc0 = _swa_scores(q_blk(0), kv[0:2 * w], bias_ref, jnp.where(first_of_seq, 0, 1))
    issue(2)

    mixed = []
    for h in range(GLA_HEADS):
        st = jnp.where(first_of_seq, 0.0, st_ref[h])
        m_h, st = _gla_mix(decays[h],
                           mix_ref[:, OFF_GQ + h * GLA_DK:OFF_GQ + (h + 1) * GLA_DK],
                           mix_ref[:, OFF_GK + h * GLA_DK:OFF_GK + (h + 1) * GLA_DK],
                           mix_ref[:, OFF_GV + h * GLA_DV:OFF_GV + (h + 1) * GLA_DV],
                           st, pm_ref, issue)
        st_ref[h] = st
        mixed.append(m_h)
        issue()

    sc1 = _swa_scores(q_blk(1), kv[w:3 * w], bias_ref, 1)
    issue()
    yb0 = _swa_values(jnp.concatenate([v_t[0], v_t[1]], axis=1), *_swa_softmax(sc0, sink_ref))
    issue()

    for h in range(GLA_HEADS):
        y_ref[slot, :, h * GLA_DV:(h + 1) * GLA_DV] = _gla_emit(
            *mixed[h], mix_ref[:, OFF_GR + h * GLA_DV:OFF_GR + (h + 1) * GLA_DV],
            gain_ref[...])
        issue()

    yb1 = _swa_values(jnp.concatenate([v_t[1], v_t[2]], axis=1), *_swa_softmax(sc1, sink_ref))
    issue()
    for r, tiles in enumerate((yb0, yb1)):
        for p, tile in enumerate(tiles):
            y_ref[slot, r * w:(r + 1) * w, YA_W + p * LANES:YA_W + (p + 1) * LANES] = tile
    issue(len(queue))


def _mixmerge(proj, gz, x2, seq_len, wd, bd, gain, sinks, wa, wb, wo):
    t = x2.shape[0]
    tm = MIX_TM
    n = t // tm
    w_all, pm = _gla_constants()
    bias = _swa_bias()
    cur = lambda col: (lambda i: (jnp.minimum(i, n - 1), col))
    lag = lambda i: (jnp.maximum(i - 1, 0), 0)
    return pl.pallas_call(
        functools.partial(_mixmerge_kernel, seq_len // tm),
        grid=(n + 1,),
        in_specs=[
            pl.BlockSpec(memory_space=pltpu.SMEM),
            pl.BlockSpec((tm, MIX_W), cur(COL_GV // MIX_W)),
            pl.BlockSpec((tm, KV_W), cur(COL_SK // KV_W)),
            pl.BlockSpec((tm, LANES), cur(0)),
            pl.BlockSpec((tm, 2 * D_MODEL), lag),
            pl.BlockSpec((tm, D_MODEL), lag),
            _resident(wd.shape), _resident(bd.shape), _resident(gain.shape),
            _resident(w_all.shape), _resident(pm.shape), _resident(bias.shape),
            _resident(wa.shape), _resident(wb.shape), _resident(wo.shape),
        ],
        out_specs=pl.BlockSpec((tm, D_MODEL), lag),
        out_shape=jax.ShapeDtypeStruct((t, D_MODEL), F32),
        scratch_shapes=[pltpu.VMEM((2, tm, YA_W + YB_W), BF16),
                        pltpu.VMEM((GLA_HEADS, GLA_DV, GLA_DK), F32),
                        pltpu.VMEM((WINDOW, KV_W), BF16)],
        compiler_params=_params("arbitrary"),
        name="mixmerge",
    )(sinks, proj, proj, gz, proj, x2, wd, bd, gain,
      jnp.asarray(w_all, BF16), jnp.asarray(pm), jnp.asarray(bias), wa, wb, wo)


MLP_TM = 1024
MLP_TF = 1024


def _mlp_kernel(h_ref, g_ref, wu_ref, wd_ref, o_ref, hn_ref):
    @pl.when(pl.program_id(1) == 0)
    def _():
        def body(r, c):
            rows = pl.ds(pl.multiple_of(r * NORM_ROWS, NORM_ROWS), NORM_ROWS)
            x = h_ref[rows, :]
            hn_ref[rows, :] = (x * _rms_scale(x) * g_ref[...]).astype(BF16)
            o_ref[rows, :] = x
            return c
        lax.fori_loop(0, h_ref.shape[0] // NORM_ROWS, body, 0)

    up = jnp.dot(hn_ref[...], wu_ref[...], preferred_element_type=F32)
    act = jnp.square(jnp.maximum(up, 0.0)).astype(BF16)
    o_ref[...] += jnp.dot(act, wd_ref[...], preferred_element_type=F32)


def _mlp(h1, g, wu, wd):
    t = h1.shape[0]
    tm = min(MLP_TM, t)
    return pl.pallas_call(
        _mlp_kernel,
        grid=(t // tm, D_FF // MLP_TF),
        in_specs=[
            pl.BlockSpec((tm, D_MODEL), lambda i, j: (i, 0)),
            pl.BlockSpec((1, D_MODEL), lambda i, j: (0, 0)),
            pl.BlockSpec((D_MODEL, MLP_TF), lambda i, j: (0, j)),
            pl.BlockSpec((MLP_TF, D_MODEL), lambda i, j: (j, 0)),
        ],
        out_specs=pl.BlockSpec((tm, D_MODEL), lambda i, j: (i, 0)),
        out_shape=jax.ShapeDtypeStruct((t, D_MODEL), F32),
        scratch_shapes=[pltpu.VMEM((tm, D_MODEL), BF16)],
        compiler_params=_params("parallel", "arbitrary"),
        name="mlp",
    )(h1, g, wu, wd)


PLE_TM = 512
PLE_PARTS = 2


def _ple_kernel(h_ref, p_ref, g_ref, gf_ref, wg_ref, wp_ref, o_ref):
    rows = h_ref.shape[0] // PLE_PARTS
    parts = [slice(k * rows, (k + 1) * rows) for k in range(PLE_PARTS)]
    hs = [h_ref[r, :] for r in parts]
    hps = [(h * _rms_scale(h) * g_ref[...]).astype(BF16) for h in hs]
    gates, embs = [], []
    for r, hp in zip(parts, hps):
        gates.append(jnp.dot(hp, wg_ref[...], preferred_element_type=F32))
        embs.append(jnp.dot(p_ref[r, :].astype(BF16), wp_ref[...],
                            preferred_element_type=F32))
    for r, h, gate, emb in zip(parts, hs, gates, embs):
        h = h + jax.nn.sigmoid(gate) * emb
        o_ref[r, :] = h * _rms_scale(h) * gf_ref[...]


def _ple(h2, p2, g, gf, wg, wp):
    t = h2.shape[0]
    tm = PLE_TM
    return pl.pallas_call(
        _ple_kernel,
        grid=(t // tm,),
        in_specs=[
            pl.BlockSpec((tm, D_MODEL), lambda i: (i, 0)),
            pl.BlockSpec((tm, PLE_DIM), lambda i: (i, 0)),
            _resident((1, D_MODEL)),
            _resident((1, D_MODEL)),
            _resident((D_MODEL, D_MODEL)),
            _resident((PLE_DIM, D_MODEL)),
        ],
        out_specs=pl.BlockSpec((tm, D_MODEL), lambda i: (i, 0)),
        out_shape=jax.ShapeDtypeStruct((t, D_MODEL), F32),
        compiler_params=_params("parallel"),
        name="ple",
    )(h2, p2, g, gf, wg, wp)


def _layer(h2d, p2d, s, norm_mix, w_in, w_decay, b_decay, gla_norm, attn_sinks,
           w_branch_a, w_branch_b, w_out, norm_mlp, w_up, w_down, norm_ple,
           w_ple_gate, w_ple_proj, final_gain):
    w_head = w_in[:, :_SRC["gz"]].astype(BF16)
    w_tail = w_in[:, _SRC["sq"]:].astype(BF16)
    w_z = jnp.pad(w_in[:, _SRC["gz"]:_SRC["sq"]],
                  ((0, 0), (0, LANES - GLA_RANK))).astype(BF16)
    proj, gz = _inproj(h2d, norm_mix.reshape(1, D_MODEL), w_head, w_tail, w_z)
    wd = jnp.pad(w_decay, ((0, LANES - GLA_RANK), (0, 0))).astype(BF16)
    h1 = _mixmerge(proj, gz, h2d, s, wd, b_decay.reshape(1, -1), gla_norm.reshape(1, GLA_DV),
                   attn_sinks, w_branch_a.astype(BF16), w_branch_b.astype(BF16),
                   w_out.astype(BF16))
    h2 = _mlp(h1, norm_mlp.reshape(1, D_MODEL), w_up.astype(BF16), w_down.astype(BF16))
    return _ple(h2, p2d, norm_ple.reshape(1, D_MODEL), final_gain,
                w_ple_gate.astype(BF16), w_ple_proj.astype(BF16))


def kernel(x, p, norm_mix, w_in, w_decay, b_decay, gla_norm, attn_sinks, w_branch_a,
           w_branch_b, w_out, norm_mlp, w_up, w_down, norm_ple, w_ple_gate, w_ple_proj,
           norm_final):
    b, s, d = x.shape
    depth = w_in.shape[0]
    assert depth == 1 and d == D_MODEL and s % MIX_TM == 0
    h = _layer(x.reshape(b * s, d), p[0].reshape(b * s, PLE_DIM), s,
               norm_mix[0], w_in[0], w_decay[0], b_decay[0], gla_norm[0], attn_sinks[0],
               w_branch_a[0], w_branch_b[0], w_out[0], norm_mlp[0], w_up[0], w_down[0],
               norm_ple[0], w_ple_gate[0], w_ple_proj[0],
               norm_final.reshape(1, D_MODEL))
    return h.reshape(b, s, d)
```
